```python
import math
import jax, jax.numpy as jnp
from jax import lax
import numpy as np

D_MODEL = 1024
BATCH = 8
SEQ = 2048
DEPTH = 2
DEC_BATCH = 128
DEC_SEQ = 8
PAST_LEN = 16384
PAGE_SIZE = 128

MIX_W = D_MODEL
GROUP_W = MIX_W // 4
HEAD_DIM = 64
N_HEADS = GROUP_W // HEAD_DIM
RWKV_DECAY_RANK = 32
RWKV_ICL_RANK = 32
RWKV_GATE_RANK = 64
RWKV_GN_EPS = 64e-5
RWKV_COLS = 3 * GROUP_W + RWKV_DECAY_RANK + RWKV_ICL_RANK + RWKV_GATE_RANK
CONV_K = 4
RG_BLOCKS = N_HEADS
RG_BLOCK_W = GROUP_W // RG_BLOCKS
RG_C = 8.0
RG_COLS = 2 * GROUP_W
GDN_COLS = 4 * GROUP_W + 2 * N_HEADS
GLA_DK = HEAD_DIM // 2
GLA_GATE_RANK = 16
GLA_TAU = 16.0
GLA_COLS = 2 * N_HEADS * GLA_DK + 2 * GROUP_W + GLA_GATE_RANK
IN_COLS = RWKV_COLS + RG_COLS + GDN_COLS + GLA_COLS
D_FF = ((8 * D_MODEL // 3 + 127) // 128) * 128
CHUNK = 64
NORM_EPS = 1e-6

IN_SPLITS = [RWKV_COLS, RWKV_COLS + RG_COLS, RWKV_COLS + RG_COLS + GDN_COLS]
RWKV_SPLITS = [GROUP_W, 2 * GROUP_W, 3 * GROUP_W, 3 * GROUP_W + RWKV_DECAY_RANK,
               3 * GROUP_W + RWKV_DECAY_RANK + RWKV_ICL_RANK]
GDN_SPLITS = [3 * GROUP_W, 4 * GROUP_W, 4 * GROUP_W + N_HEADS]
GLA_SPLITS = [N_HEADS * GLA_DK, 2 * N_HEADS * GLA_DK, 2 * N_HEADS * GLA_DK + GROUP_W,
              2 * N_HEADS * GLA_DK + 2 * GROUP_W]

kernel_name = 'hybrid_rwkv7_rglru_gdn_gla_macaron_step'

F32 = jnp.float32


def _rmsnorm(x, w):
    xf = x.astype(F32)
    y = xf * lax.rsqrt(jnp.mean(xf * xf, axis=-1, keepdims=True) + NORM_EPS)
    return (y * w.astype(F32)).astype(x.dtype)


def _head_rmsnorm(x, w):
    return x * lax.rsqrt(jnp.mean(x * x, axis=-1, keepdims=True) + NORM_EPS) * w


def _l2norm(x):
    return x * lax.rsqrt(jnp.sum(x * x, axis=-1, keepdims=True) + 1e-12)


def _heads(t, d):
    return t.reshape(t.shape[0], t.shape[1], -1, d)


def _swiglu(x, w_gate, w_up, w_down):
    return (jax.nn.silu(x @ w_gate) * (x @ w_up)) @ w_down


def _causal_conv(x, buf, w):
    xp = jnp.concatenate([buf.astype(x.dtype), x], axis=1)
    y = lax.conv_general_dilated(xp, w.astype(x.dtype)[:, None, :], window_strides=(1,), padding='VALID',
                                 dimension_numbers=('NWC', 'WIO', 'NWC'), feature_group_count=x.shape[-1])
    return y, xp[:, -(CONV_K - 1):]


def _linear_scan(a, b, h0):
    b = b.at[:, 0].add(a[:, 0] * h0)
    def comb(l, r):
        return (l[0] * r[0], r[0] * l[1] + r[1])
    _, h = lax.associative_scan(comb, (a, b), axis=1)
    return h


def _chunk_len(T):
    return CHUNK if T % CHUNK == 0 else T


def _to_chunks(x, L):
    B, T, H, d = x.shape
    return x.reshape(B, T // L, L, H, d).transpose(1, 0, 3, 2, 4)


def _from_chunks(y):
    n, B, H, L, d = y.shape
    return y.transpose(1, 0, 3, 2, 4).reshape(B, n * L, H, d)


def _rwkv7_scan(r, w, k, v, kk, a, S0):
    def step(S, inp):
        r_t, w_t, k_t, v_t, kk_t, a_t = inp
        sa = jnp.einsum('bhij,bhj->bhi', S, -kk_t)
        S = (S * w_t[:, :, None, :] + sa[..., None] * (kk_t * a_t)[:, :, None, :]
             + v_t[..., None] * k_t[:, :, None, :])
        return S, jnp.einsum('bhij,bhj->bhi', S, r_t)
    xs = tuple(t.transpose(1, 0, 2, 3) for t in (r, w, k, v, kk, a))
    S, ys = lax.scan(step, S0, xs)
    return ys.transpose(1, 0, 2, 3), S


def _gated_delta_chunked(q, k, v, beta, g, S0):
    L = _chunk_len(q.shape[1])
    dk = q.shape[-1]
    qc, kc, vc = (_to_chunks(t, L) for t in (q, k, v))
    bc = _to_chunks(beta[..., None], L)[..., 0]
    gc = _to_chunks(g[..., None], L)[..., 0]
    strict = jnp.tril(jnp.ones((L, L), bool), -1)
    incl = jnp.tril(jnp.ones((L, L), bool))
    eye = jnp.eye(L, dtype=F32)

    def step(S, inp):
        qi, ki, vi, bi, gi = inp
        G = jnp.cumsum(gi, axis=-1)
        diff = G[..., :, None] - G[..., None, :]
        eG = jnp.exp(G)
        kk = jnp.einsum('bhid,bhjd->bhij', ki, ki)
        A = bi[..., :, None] * kk * jnp.exp(jnp.where(strict, diff, -jnp.inf))
        rhs = jnp.concatenate([(bi * eG)[..., None] * ki, bi[..., None] * vi], axis=-1)
        sol = lax.linalg.triangular_solve(eye + A, rhs, left_side=True, lower=True, unit_diagonal=True)
        W, U = sol[..., :dk], sol[..., dk:]
        v_new = U - jnp.einsum('bhld,bhde->bhle', W, S)
        attn = jnp.einsum('bhid,bhjd->bhij', qi, ki) * jnp.exp(jnp.where(incl, diff, -jnp.inf))
        o = jnp.einsum('bhld,bhde->bhle', qi * eG[..., None], S) + jnp.einsum('bhij,bhje->bhie', attn, v_new)
        G_last = G[..., -1:]
        S_new = (jnp.exp(G_last)[..., None] * S
                 + jnp.einsum('bhld,bhle->bhde', ki * jnp.exp(G_last - G)[..., None], v_new))
        return S_new, o

    S, oc = lax.scan(step, S0, (qc, kc, vc, bc, gc))
    return _from_chunks(oc), S


def _gla_chunked(q, k, v, gk, S0):
    L = _chunk_len(q.shape[1])
    qc, kc, vc, gc = (_to_chunks(t, L) for t in (q, k, v, gk))
    incl = jnp.tril(jnp.ones((L, L), bool))[:, :, None]

    def step(S, inp):
        qi, ki, vi, gi = inp
        Bc = jnp.cumsum(gi, axis=2)
        diff = Bc[:, :, :, None, :] - Bc[:, :, None, :, :]
        decay = jnp.exp(jnp.where(incl, diff, -jnp.inf))
        attn = jnp.einsum('bhid,bhjd,bhijd->bhij', qi, ki, decay)
        o = jnp.einsum('bhid,bhde->bhie', qi * jnp.exp(Bc), S) + jnp.einsum('bhij,bhje->bhie', attn, vi)
        B_last = Bc[:, :, -1:, :]
        S_new = (jnp.exp(B_last[:, :, 0, :])[..., None] * S
                 + jnp.einsum('bhld,bhle->bhde', ki * jnp.exp(B_last - Bc), vi))
        return S_new, o

    S, oc = lax.scan(step, S0, (qc, kc, vc, gc))
    return _from_chunks(oc), S


def _rwkv7_group(p, shift0, S0, lp):
    B, T, _ = p.shape
    prev = jnp.concatenate([shift0.astype(F32)[:, None], p[:, :-1]], axis=1)
    xm = p + (prev - p) * lp['rwkv_mu']
    r, k, v, xw, xa, xg = jnp.split(xm, RWKV_SPLITS, axis=-1)
    w = -jax.nn.softplus(-(lp['rwkv_w0'] + jnp.tanh(xw) @ lp['rwkv_w2'])) - 0.5
    decay = jnp.exp(-jnp.exp(w))
    a = jax.nn.sigmoid(lp['rwkv_a0'] + xa @ lp['rwkv_a2'])
    g = jax.nn.sigmoid(xg) @ lp['rwkv_g2']
    kk = _l2norm(_heads(k * lp['rwkv_k_k'], HEAD_DIM))
    k = k * (1.0 + (a - 1.0) * lp['rwkv_k_a'])
    rh, kh, vh, ah, wh = (_heads(t, HEAD_DIM) for t in (r, k, v, a, decay))
    y, S = _rwkv7_scan(rh, wh, kh, vh, kk, ah, S0.astype(F32))
    mu = jnp.mean(y, axis=-1, keepdims=True)
    var = jnp.mean(jnp.square(y - mu), axis=-1, keepdims=True)
    y = ((y - mu) * lax.rsqrt(var + RWKV_GN_EPS)).reshape(B, T, GROUP_W) * lp['rwkv_ln_w'] + lp['rwkv_ln_b']
    bonus = jnp.sum(rh * kh * lp['rwkv_r_k'], axis=-1, keepdims=True) * vh
    out = (y + bonus.reshape(B, T, GROUP_W)) * g
    return out, p[:, -1], S


def _rglru_group(p, conv0, h0, lp):
    B, T, _ = p.shape
    xb, gate = jnp.split(p, [GROUP_W], axis=-1)
    xc, conv_new = _causal_conv(xb, conv0, lp['rg_conv_w'])
    xc = xc + lp['rg_conv_b']
    xh = xc.reshape(B, T, RG_BLOCKS, RG_BLOCK_W)
    r = jax.nn.sigmoid(jnp.einsum('btni,nij->btnj', xh, lp['rg_wa']).reshape(B, T, GROUP_W) + lp['rg_ba'])
    i = jax.nn.sigmoid(jnp.einsum('btni,nij->btnj', xh, lp['rg_wx']).reshape(B, T, GROUP_W) + lp['rg_bx'])
    log_a = -RG_C * r * jax.nn.softplus(-lp['rg_lambda'].astype(F32))
    a = jnp.exp(log_a)
    b = jnp.sqrt(-jnp.expm1(2.0 * log_a)) * (i * xc)
    h = _linear_scan(a, b, h0.astype(F32))
    return h * jax.nn.gelu(gate), conv_new, h[:, -1]


def _gdn_group(p, conv0, S0, lp):
    B, T, _ = p.shape
    qkv, z, b, a = jnp.split(p, GDN_SPLITS, axis=-1)
    qkv, conv_new = _causal_conv(qkv, conv0, lp['gdn_conv_w'])
    q, k, v = jnp.split(jax.nn.silu(qkv), [GROUP_W, 2 * GROUP_W], axis=-1)
    q = _l2norm(_heads(q, HEAD_DIM)) * (HEAD_DIM ** -0.5)
    k = _l2norm(_heads(k, HEAD_DIM))
    v = _heads(v, HEAD_DIM)
    beta = jax.nn.sigmoid(b)
    g = -jnp.exp(lp['gdn_A_log'].astype(F32)) * jax.nn.softplus(a + lp['gdn_dt_bias'])
    o, S = _gated_delta_chunked(q, k, v, beta, g, S0.astype(F32))
    o = _head_rmsnorm(o, lp['gdn_norm_w']) * jax.nn.silu(_heads(z, HEAD_DIM))
    return o.reshape(B, T, GROUP_W), conv_new, S


def _gla_group(p, S0, lp):
    B, T, _ = p.shape
    q, k, v, gt, gd = jnp.split(p, GLA_SPLITS, axis=-1)
    gk = jax.nn.log_sigmoid(gd @ lp['gla_g2'] + lp['gla_gb']) / GLA_TAU
    q = _heads(q, GLA_DK) * (GLA_DK ** -0.5)
    o, S = _gla_chunked(q, _heads(k, GLA_DK), _heads(v, HEAD_DIM), _heads(gk, GLA_DK), S0.astype(F32))
    o = _head_rmsnorm(o, lp['gla_norm_w']) * jax.nn.silu(_heads(gt, HEAD_DIM))
    return o.reshape(B, T, GROUP_W), S


def _trunk_layer(x, lp, st):
    rw_shift, rw_S, rg_conv, rg_h, gdn_conv, gdn_S, gla_S = st
    x = x + 0.5 * _swiglu(_rmsnorm(x, lp['norm_ffn1']), lp['ffn1_gate'], lp['ffn1_up'], lp['ffn1_down'])
    h = _rmsnorm(x, lp['norm_mix'])
    proj = (h @ lp['w_in']).astype(F32)
    p_rw, p_rg, p_gdn, p_gla = jnp.split(proj, IN_SPLITS, axis=-1)
    o_rw, rw_shift, rw_S = _rwkv7_group(p_rw, rw_shift, rw_S, lp)
    o_rg, rg_conv, rg_h = _rglru_group(p_rg, rg_conv, rg_h, lp)
    o_gdn, gdn_conv, gdn_S = _gdn_group(p_gdn, gdn_conv, gdn_S, lp)
    o_gla, gla_S = _gla_group(p_gla, gla_S, lp)
    mixed = jnp.concatenate([o_rw, o_rg, o_gdn, o_gla], axis=-1).astype(x.dtype)
    x = x + mixed @ lp['w_out']
    x = x + 0.5 * _swiglu(_rmsnorm(x, lp['norm_ffn2']), lp['ffn2_gate'], lp['ffn2_up'], lp['ffn2_down'])
    return x, (rw_shift, rw_S, rg_conv, rg_h, gdn_conv, gdn_S, gla_S)


def _zero_state(batch, dtype):
    return (jnp.zeros((batch, RWKV_COLS), dtype),
            jnp.zeros((batch, N_HEADS, HEAD_DIM, HEAD_DIM), dtype),
            jnp.zeros((batch, CONV_K - 1, GROUP_W), dtype),
            jnp.zeros((batch, GROUP_W), dtype),
            jnp.zeros((batch, CONV_K - 1, 3 * GROUP_W), dtype),
            jnp.zeros((batch, N_HEADS, HEAD_DIM, HEAD_DIM), dtype),
            jnp.zeros((batch, N_HEADS, GLA_DK, HEAD_DIM), dtype))


def _run_trunk(x, layer_states, lps, norm_final):
    new = []
    for i in range(DEPTH):
        x, st = _trunk_layer(x, lps[i], layer_states[i])
        new.append(st)
    stacked = tuple(jnp.stack([new[i][j] for i in range(DEPTH)]).astype(x.dtype) for j in range(len(new[0])))
    return _rmsnorm(x, norm_final), stacked


def setup_inputs(seed: int = 0) -> dict:
    key = jax.random.key(seed)
    ks = iter(jax.random.split(key, 96))
    def nrm(shape, scale):
        return jax.random.normal(next(ks), shape, F32) * scale
    def uni(shape, lo, hi):
        return jax.random.uniform(next(ks), shape, F32, lo, hi)
    def gains(shape):
        return 1.0 + nrm(shape, 0.02)
    L, G, H, N = DEPTH, GROUP_W, N_HEADS, HEAD_DIM
    lam_a = uni((L, G), 0.9, 0.999)
    dt = jnp.exp(uni((L, H), math.log(1e-3), math.log(1e-1)))
    return {
        'x_prompt': nrm((BATCH, SEQ, D_MODEL), 1.0),
        'x_sample': nrm((DEC_BATCH, DEC_SEQ, D_MODEL), 1.0),
        'state_rwkv_shift': nrm((L, DEC_BATCH, RWKV_COLS), 1.0),
        'state_rwkv_wkv': nrm((L, DEC_BATCH, H, N, N), 0.3),
        'state_rglru_conv': nrm((L, DEC_BATCH, CONV_K - 1, G), 1.0),
        'state_rglru_h': nrm((L, DEC_BATCH, G), 0.5),
        'state_gdn_conv': nrm((L, DEC_BATCH, CONV_K - 1, 3 * G), 1.0),
        'state_gdn_S': nrm((L, DEC_BATCH, H, N, N), 0.3),
        'state_gla_S': nrm((L, DEC_BATCH, H, GLA_DK, N), 0.3),
        'norm_ffn1': gains((L, D_MODEL)),
        'ffn1_gate': nrm((L, D_MODEL, D_FF), D_MODEL ** -0.5),
        'ffn1_up': nrm((L, D_MODEL, D_FF), D_MODEL ** -0.5),
        'ffn1_down': nrm((L, D_FF, D_MODEL), D_FF ** -0.5),
        'norm_mix': gains((L, D_MODEL)),
        'w_in': nrm((L, D_MODEL, IN_COLS), D_MODEL ** -0.5),
        'w_out': nrm((L, MIX_W, D_MODEL), MIX_W ** -0.5),
        'rwkv_mu': uni((L, RWKV_COLS), 0.0, 1.0),
        'rwkv_w0': uni((L, G), -6.0, 1.0),
        'rwkv_w2': nrm((L, RWKV_DECAY_RANK, G), 0.1 * RWKV_DECAY_RANK ** -0.5),
        'rwkv_a0': nrm((L, G), 0.1),
        'rwkv_a2': nrm((L, RWKV_ICL_RANK, G), 0.1 * RWKV_ICL_RANK ** -0.5),
        'rwkv_g2': nrm((L, RWKV_GATE_RANK, G), RWKV_GATE_RANK ** -0.5),
        'rwkv_k_k': 0.85 + nrm((L, G), 0.02),
        'rwkv_k_a': gains((L, G)),
        'rwkv_r_k': nrm((L, H, N), 0.1),
        'rwkv_ln_w': gains((L, G)),
        'rwkv_ln_b': nrm((L, G), 0.02),
        'rg_conv_w': nrm((L, CONV_K, G), CONV_K ** -0.5),
        'rg_conv_b': nrm((L, G), 0.02),
        'rg_wa': nrm((L, RG_BLOCKS, RG_BLOCK_W, RG_BLOCK_W), RG_BLOCK_W ** -0.5),
        'rg_ba': nrm((L, G), 0.02),
        'rg_wx': nrm((L, RG_BLOCKS, RG_BLOCK_W, RG_BLOCK_W), RG_BLOCK_W ** -0.5),
        'rg_bx': nrm((L, G), 0.02),
        'rg_lambda': jnp.log(lam_a) - jnp.log1p(-lam_a),
        'gdn_conv_w': nrm((L, CONV_K, 3 * G), CONV_K ** -0.5),
        'gdn_A_log': jnp.log(uni((L, H), 1.0, 16.0)),
        'gdn_dt_bias': dt + jnp.log(-jnp.expm1(-dt)),
        'gdn_norm_w': gains((L, N)),
        'gla_g2': nrm((L, GLA_GATE_RANK, H * GLA_DK), GLA_GATE_RANK ** -0.5),
        'gla_gb': nrm((L, H * GLA_DK), 0.1),
        'gla_norm_w': gains((L, N)),
        'norm_ffn2': gains((L, D_MODEL)),
        'ffn2_gate': nrm((L, D_MODEL, D_FF), D_MODEL ** -0.5),
        'ffn2_up': nrm((L, D_MODEL, D_FF), D_MODEL ** -0.5),
        'ffn2_down': nrm((L, D_FF, D_MODEL), D_FF ** -0.5),
        'norm_final': gains((D_MODEL,)),
    }


def reference(x_prompt, x_sample, state_rwkv_shift, state_rwkv_wkv, state_rglru_conv, state_rglru_h,
              state_gdn_conv, state_gdn_S, state_gla_S, norm_ffn1, ffn1_gate, ffn1_up, ffn1_down,
              norm_mix, w_in, w_out, rwkv_mu, rwkv_w0, rwkv_w2, rwkv_a0, rwkv_a2, rwkv_g2, rwkv_k_k,
              rwkv_k_a, rwkv_r_k, rwkv_ln_w, rwkv_ln_b, rg_conv_w, rg_conv_b, rg_wa, rg_ba, rg_wx, rg_bx,
              rg_lambda, gdn_conv_w, gdn_A_log, gdn_dt_bias, gdn_norm_w, gla_g2, gla_gb, gla_norm_w,
              norm_ffn2, ffn2_gate, ffn2_up, ffn2_down, norm_final):
    stacked = dict(norm_ffn1=norm_ffn1, ffn1_gate=ffn1_gate, ffn1_up=ffn1_up, ffn1_down=ffn1_down,
                   norm_mix=norm_mix, w_in=w_in, w_out=w_out, rwkv_mu=rwkv_mu, rwkv_w0=rwkv_w0,
                   rwkv_w2=rwkv_w2, rwkv_a0=rwkv_a0, rwkv_a2=rwkv_a2, rwkv_g2=rwkv_g2, rwkv_k_k=rwkv_k_k,
                   rwkv_k_a=rwkv_k_a, rwkv_r_k=rwkv_r_k, rwkv_ln_w=rwkv_ln_w, rwkv_ln_b=rwkv_ln_b,
                   rg_conv_w=rg_conv_w, rg_conv_b=rg_conv_b, rg_wa=rg_wa, rg_ba=rg_ba, rg_wx=rg_wx,
                   rg_bx=rg_bx, rg_lambda=rg_lambda, gdn_conv_w=gdn_conv_w, gdn_A_log=gdn_A_log,
                   gdn_dt_bias=gdn_dt_bias, gdn_norm_w=gdn_norm_w, gla_g2=gla_g2, gla_gb=gla_gb,
                   gla_norm_w=gla_norm_w, norm_ffn2=norm_ffn2, ffn2_gate=ffn2_gate, ffn2_up=ffn2_up,
                   ffn2_down=ffn2_down)
    lps = [{name: arr[i] for name, arr in stacked.items()} for i in range(DEPTH)]

    prompt_states = [_zero_state(x_prompt.shape[0], x_prompt.dtype) for _ in range(DEPTH)]
    y_prompt, new_p = _run_trunk(x_prompt, prompt_states, lps, norm_final)
    p_rwkv_shift, p_rwkv_wkv, p_rglru_conv, p_rglru_h, p_gdn_conv, p_gdn_S, p_gla_S = new_p

    sample_in = (state_rwkv_shift, state_rwkv_wkv, state_rglru_conv, state_rglru_h,
                 state_gdn_conv, state_gdn_S, state_gla_S)
    sample_states = [tuple(s[i] for s in sample_in) for i in range(DEPTH)]
    y_sample, new_s = _run_trunk(x_sample, sample_states, lps, norm_final)
    s_rwkv_shift, s_rwkv_wkv, s_rglru_conv, s_rglru_h, s_gdn_conv, s_gdn_S, s_gla_S = new_s

    return (y_prompt, y_sample,
            p_rwkv_shift, p_rwkv_wkv, p_rglru_conv, p_rglru_h, p_gdn_conv, p_gdn_S, p_gla_S,
            s_rwkv_shift, s_rwkv_wkv, s_rglru_conv, s_rglru_h, s_gdn_conv, s_gdn_S, s_gla_S)
```

```python
import functools

import jax
import jax.numpy as jnp
from jax import lax
from jax.experimental import pallas as pl
from jax.experimental.pallas import tpu as pltpu

F32 = jnp.float32
BF16 = jnp.bfloat16

D_MODEL = 1024
GROUP_W = 256
HEAD_DIM = 64
N_HEADS = 4
GLA_DK = 32
GLA_QK_W = N_HEADS * GLA_DK
RWKV_COLS = 896
CONV_K = 4
D_FF = 2816
NORM_EPS = 1e-6
RWKV_GN_EPS = 64e-5
RG_C = 8.0
GLA_TAU = 16.0
LANE = 128
SUBLANE = 8
CARRY_ROWS = SUBLANE
VMEM_LIMIT = 56 * 1024 * 1024

GDN_P_W = 3 * GROUP_W + GROUP_W + LANE
GLA_P_W = 2 * GLA_QK_W + 2 * GROUP_W + LANE
RG_P_W = 2 * GROUP_W

FFN_TM = 512
FFN_TF = 1408
PROMPT_CHUNK = 64
PAD_CHUNK = 32
RG_CHUNK = 256


def _dg(a, b, ca, cb):
    return lax.dot_general(a, b, (((ca,), (cb,)), ((), ())), preferred_element_type=F32)


def _split2(x):
    hi = x.astype(BF16)
    lo = (x - hi.astype(F32)).astype(BF16)
    return hi, lo


def _split3(x):
    hi = x.astype(BF16)
    r1 = x - hi.astype(F32)
    mid = r1.astype(BF16)
    lo = (r1 - mid.astype(F32)).astype(BF16)
    return hi, mid, lo


def _mm(a, b, ca=1, cb=0, passes=1):
    if passes == 1:
        return _dg(a.astype(BF16), b.astype(BF16), ca, cb)
    a1, a2 = _split2(a)
    b1, b2 = _split2(b)
    return _dg(a1, b1, ca, cb) + (_dg(a1, b2, ca, cb) + _dg(a2, b1, ca, cb))


def _mm_xc(x, c, ca=1, cb=0):
    x1, x2, x3 = _split3(x)
    return _dg(x1, c, ca, cb) + (_dg(x2, c, ca, cb) + _dg(x3, c, ca, cb))


def _mm_cx(c, x, ca=1, cb=0):
    x1, x2, x3 = _split3(x)
    return _dg(c, x1, ca, cb) + (_dg(c, x2, ca, cb) + _dg(c, x3, ca, cb))


def _iota(shape, dim):
    return lax.broadcasted_iota(jnp.int32, shape, dim)


def _log2(n):
    assert n & (n - 1) == 0
    return n.bit_length() - 1


def _head_masks(width, head_w):
    lane = _iota((1, width), 1)
    return [((lane >> _log2(head_w)) == h).astype(F32) for h in range(width // head_w)]


def _stack(x, masks):
    return jnp.concatenate([x * m for m in masks], axis=0)


def _unstack(xs, masks, rows):
    out = xs[0:rows] * masks[0]
    for h in range(1, len(masks)):
        out = out + xs[h * rows:(h + 1) * rows] * masks[h]
    return out


def _block_ones(rows, cols, rblk, cblk):
    r = _iota((rows, cols), 0) >> _log2(rblk)
    c = _iota((rows, cols), 1) >> _log2(cblk)
    return (r == c).astype(F32).astype(BF16)


def _tri_incl(n):
    return (_iota((n, n), 1) <= _iota((n, n), 0)).astype(F32).astype(BF16)


def _softplus(x):
    return jnp.maximum(x, 0.0) + jnp.log1p(jnp.exp(-jnp.abs(x)))


def _sigmoid(x):
    return 1.0 / (1.0 + jnp.exp(-x))


def _silu(x):
    return x * _sigmoid(x)


def _rmsnorm(x, w):
    return x * lax.rsqrt(jnp.mean(x * x, axis=-1, keepdims=True) + NORM_EPS) * w


def _shift_rows(x, ext, d):
    rows = x.shape[0]
    xs = pltpu.roll(x, d, 0)
    fix = pltpu.roll(ext, d, 0)
    first = jnp.where(_iota((CARRY_ROWS, x.shape[1]), 0) < d, fix, xs[0:CARRY_ROWS])
    if rows == CARRY_ROWS:
        return first
    return jnp.concatenate([first, xs[CARRY_ROWS:]], axis=0)


def _pad_rows(x, rows):
    if x.shape[0] == rows:
        return x
    return jnp.concatenate([x, jnp.zeros((rows - x.shape[0], x.shape[1]), x.dtype)], axis=0)


def _unit_lower_inverse(a, chunk, passes):
    n = a.shape[0]
    row = _iota((n, n), 0)
    col = _iota((n, n), 1)
    eye = (row == col).astype(F32)
    base = _log2(SUBLANE)
    a0 = jnp.where((row >> base) == (col >> base), a, 0.0)
    a2 = _mm(a0, a0, passes=passes)
    a4 = _mm(a2, a2, passes=passes)
    t = _mm(eye - a0, eye + a2, passes=passes)
    t = _mm(t, eye + a4, passes=passes)
    size = SUBLANE
    while size < chunk:
        lo = _log2(size)
        off = jnp.where(((row >> (lo + 1)) == (col >> (lo + 1))) & ((row >> lo) != (col >> lo)), a, 0.0)
        t = t - _mm(_mm(t, off, passes=passes), t, passes=passes)
        size *= 2
    return t


def _swiglu_half_step(x, nw, wg_ref, wu_ref, wd_ref):
    hb = _rmsnorm(x, nw).astype(BF16)
    acc = jnp.zeros_like(x)
    for c in range(D_FF // FFN_TF):
        sl = slice(c * FFN_TF, (c + 1) * FFN_TF)
        g = _dg(hb, wg_ref[:, sl], 1, 0)
        u = _dg(hb, wu_ref[:, sl], 1, 0)
        acc = acc + _dg((_silu(g) * u).astype(BF16), wd_ref[sl, :], 1, 0)
    return x + 0.5 * acc


def _ffn_in_body(x_ref, n1_ref, wg_ref, wu_ref, wd_ref, nm_ref, win_ref,
                 x_out, prw_out, prg_out, pgdn_out, pgla_out):
    x1 = _swiglu_half_step(x_ref[...], n1_ref[...], wg_ref, wu_ref, wd_ref)
    x_out[...] = x1
    hb = _rmsnorm(x1, nm_ref[...]).astype(BF16)
    off = 0
    for out in (prw_out, prg_out, pgdn_out, pgla_out):
        w = out.shape[-1]
        out[...] = _dg(hb, win_ref[:, off:off + w], 1, 0)
        off += w


def _out_ffn_body(x_ref, orw_ref, org_ref, ogdn_ref, ogla_ref, wo_ref, n2_ref, wg_ref, wu_ref, wd_ref,
                  nf_ref, y_out, *, final_norm):
    mixed = jnp.concatenate([orw_ref[...], org_ref[...], ogdn_ref[...], ogla_ref[...]], axis=-1)
    x2 = x_ref[...] + _dg(mixed.astype(BF16), wo_ref[...], 1, 0)
    y = _swiglu_half_step(x2, n2_ref[...], wg_ref, wu_ref, wd_ref)
    if final_norm:
        y = _rmsnorm(y, nf_ref[...])
    y_out[...] = y


def _resident(shape):
    return pl.BlockSpec(shape, lambda *_: (0,) * len(shape), pipeline_mode=pl.Buffered(1))


def _row_spec(width):
    return pl.BlockSpec((FFN_TM, width), lambda i: (i, 0))


def _dense_params():
    return pltpu.CompilerParams(dimension_semantics=("parallel",), vmem_limit_bytes=VMEM_LIMIT)


def _ffn_in(x, lp):
    rows = x.shape[0]
    widths = (RWKV_COLS, RG_P_W, GDN_P_W, GLA_P_W)
    return pl.pallas_call(
        _ffn_in_body,
        grid=(rows // FFN_TM,),
        in_specs=[_row_spec(D_MODEL), _resident((1, D_MODEL)),
                  _resident((D_MODEL, D_FF)), _resident((D_MODEL, D_FF)), _resident((D_FF, D_MODEL)),
                  _resident((1, D_MODEL)), _resident((D_MODEL, sum(widths)))],
        out_specs=[_row_spec(D_MODEL)] + [_row_spec(w) for w in widths],
        out_shape=[jax.ShapeDtypeStruct((rows, D_MODEL), F32)]
        + [jax.ShapeDtypeStruct((rows, w), F32) for w in widths],
        compiler_params=_dense_params(),
    )(x, lp["norm_ffn1"], lp["ffn1_gate"], lp["ffn1_up"], lp["ffn1_down"], lp["norm_mix"], lp["w_in"])


def _out_ffn(x, outs, lp, norm_final, final_norm):
    rows = x.shape[0]
    return pl.pallas_call(
        functools.partial(_out_ffn_body, final_norm=final_norm),
        grid=(rows // FFN_TM,),
        in_specs=[_row_spec(D_MODEL)] + [_row_spec(GROUP_W)] * 4
        + [_resident((D_MODEL, D_MODEL)), _resident((1, D_MODEL)),
           _resident((D_MODEL, D_FF)), _resident((D_MODEL, D_FF)), _resident((D_FF, D_MODEL)),
           _resident((1, D_MODEL))],
        out_specs=_row_spec(D_MODEL),
        out_shape=jax.ShapeDtypeStruct((rows, D_MODEL), F32),
        compiler_params=_dense_params(),
    )(x, *outs, lp["w_out"], lp["norm_ffn2"], lp["ffn2_gate"], lp["ffn2_up"], lp["ffn2_down"], norm_final)


MIX_PASSES = 3


def _rwkv_body(p_ref, sh0_ref, s0_ref, mu_ref, w0_ref, w2_ref, a0_ref, a2_ref, g2_ref, kk_ref, ka_ref,
               rk_ref, lnw_ref, lnb_ref, o_ref, sh_ref, s_ref, *, chunk, valid):
    @pl.when(pl.program_id(1) == 0)
    def _():
        s_ref[...] = s0_ref[...]
        sh_ref[...] = sh0_ref[...]

    P = MIX_PASSES
    L = chunk
    n = N_HEADS * L
    p = p_ref[...]
    prev = _shift_rows(p, sh_ref[...], 1)
    sh_ref[...] = p[valid - CARRY_ROWS:valid]
    xm = _pad_rows(p + (prev - p) * mu_ref[...], L)
    r = xm[:, 0:GROUP_W]
    k = xm[:, GROUP_W:2 * GROUP_W]
    v = xm[:, 2 * GROUP_W:3 * GROUP_W]
    x_lr = xm[:, 3 * GROUP_W:RWKV_COLS]
    w_log = -_softplus(-(w0_ref[...] + _mm(jnp.tanh(x_lr), w2_ref[...]))) - 0.5
    lw = -jnp.exp(w_log)
    a_icl = _sigmoid(a0_ref[...] + _mm(x_lr, a2_ref[...]))
    gate = _mm(_sigmoid(x_lr), g2_ref[...])
    bones = _block_ones(GROUP_W, GROUP_W, HEAD_DIM, HEAD_DIM)
    kk = k * kk_ref[...]
    kk = kk * lax.rsqrt(_mm_xc(kk * kk, bones) + 1e-12)
    k = k * (1.0 + (a_icl - 1.0) * ka_ref[...])
    b_vec = kk * a_icl
    if valid < L:
        live = _iota((L, GROUP_W), 0) < valid
        lw = jnp.where(live, lw, 0.0)
        b_vec = jnp.where(live, b_vec, 0.0)
        k = jnp.where(live, k, 0.0)

    lg = _mm_cx(_tri_incl(L), lw)
    dec_in = jnp.exp(lg)
    dec_out = jnp.exp(-lg)
    masks = _head_masks(GROUP_W, HEAD_DIM)
    a_s = _stack(-kk * jnp.exp(lg - lw), masks)
    b_s = _stack(b_vec * dec_out, masks)
    k_s = _stack(k * dec_out, masks)
    r_s = _stack(r * dec_in, masks)
    v_s = _stack(v, masks)
    bk_s = jnp.concatenate([b_s, k_s], axis=0)

    row = _iota((n, n), 0)
    col = _iota((n, n), 1)
    same = (row >> _log2(L)) == (col >> _log2(L))
    strict = same & (col < row)
    incl = same & (col <= row)
    a_bk = _mm(a_s, bk_s, 1, 1, P)
    a_ab = jnp.where(strict, a_bk[:, 0:n], 0.0)
    a_ak = jnp.where(strict, a_bk[:, n:2 * n], 0.0)
    r_bk = _mm(r_s, bk_s, 1, 1, P)
    m_rb = jnp.where(incl, r_bk[:, 0:n], 0.0)
    m_rk = jnp.where(incl, r_bk[:, n:2 * n], 0.0)

    s = s_ref[...]
    ar_z = _mm(jnp.concatenate([a_s, r_s], axis=0), s, 1, 1, P)
    t_inv = _unit_lower_inverse(-a_ab, L, P)
    u = _mm(t_inv, ar_z[0:n] + _mm(a_ak, v_s, passes=P), passes=P)
    uv = jnp.concatenate([u, v_s], axis=0)
    y_s = ar_z[n:2 * n] + _mm(jnp.concatenate([m_rb, m_rk], axis=1), uv, passes=P)
    y = _unstack(y_s, masks, L)

    lg_last = lg[L - 1:L]
    carry = jnp.exp(lg_last - lg)
    bk_hat = jnp.concatenate([_stack(b_vec * carry, masks), _stack(k * carry, masks)], axis=0)
    s_ref[...] = s * jnp.exp(lg_last) + _mm(uv, bk_hat, 0, 0, P)

    inv_n = 1.0 / HEAD_DIM
    mean = _mm_xc(y, bones) * inv_n
    yc = y - mean
    var = _mm_xc(yc * yc, bones) * inv_n
    yn = yc * lax.rsqrt(var + RWKV_GN_EPS) * lnw_ref[...] + lnb_ref[...]
    bonus = _mm_xc(r * k * rk_ref[...], bones) * v
    o_ref[...] = ((yn + bonus) * gate)[0:valid]


def _gdn_body(p_ref, cv0_ref, s0_ref, cw_ref, alog_ref, dtb_ref, nw_ref, o_ref, cv_ref, s_ref, *, chunk, valid):
    @pl.when(pl.program_id(1) == 0)
    def _():
        s_ref[...] = s0_ref[...]
        cv_ref[...] = cv0_ref[...]

    P = MIX_PASSES
    L = chunk
    n = N_HEADS * L
    p = p_ref[...]
    qkv_in = p[:, 0:3 * GROUP_W]
    z = p[:, 3 * GROUP_W:4 * GROUP_W]
    ba = p[:, 4 * GROUP_W:GDN_P_W]
    ext = cv_ref[...]
    cw = cw_ref[...]
    conv = cw[CONV_K - 1:CONV_K] * qkv_in
    for d in range(1, CONV_K):
        conv = conv + cw[CONV_K - 1 - d:CONV_K - d] * _shift_rows(qkv_in, ext, d)
    cv_ref[...] = qkv_in[valid - CARRY_ROWS:valid]
    qkv = _silu(conv)
    bones = _block_ones(GROUP_W, GROUP_W, HEAD_DIM, HEAD_DIM)
    q = qkv[:, 0:GROUP_W]
    k = qkv[:, GROUP_W:2 * GROUP_W]
    v = qkv[:, 2 * GROUP_W:3 * GROUP_W]
    q = q * lax.rsqrt(_mm_xc(q * q, bones) + 1e-12) * (HEAD_DIM ** -0.5)
    k = k * lax.rsqrt(_mm_xc(k * k, bones) + 1e-12)
    src = _iota((LANE, GROUP_W), 0)
    dst = _iota((LANE, GROUP_W), 1) >> _log2(HEAD_DIM)
    spread_b = (src == dst).astype(F32).astype(BF16)
    spread_a = (src == dst + N_HEADS).astype(F32).astype(BF16)
    beta = _sigmoid(_mm_xc(ba, spread_b))
    g = -jnp.exp(alog_ref[...]) * _softplus(_mm_xc(ba, spread_a) + dtb_ref[...])
    q, k, v, z, beta, g = (_pad_rows(t, L) for t in (q, k, v, z, beta, g))
    if valid < L:
        live = _iota((L, GROUP_W), 0) < valid
        beta = jnp.where(live, beta, 0.0)
        g = jnp.where(live, g, 0.0)

    gc = _mm_cx(_tri_incl(L), g)
    eg = jnp.exp(gc)
    masks = _head_masks(GROUP_W, HEAD_DIM)
    k_s = _stack(k, masks)
    q_s = _stack(q, masks)
    inv_w = 1.0 / HEAD_DIM
    g_col = jnp.sum(_stack(gc, masks), axis=-1, keepdims=True) * inv_w
    beta_col = jnp.sum(_stack(beta, masks), axis=-1, keepdims=True) * inv_w
    g_row = jnp.transpose(jnp.broadcast_to(g_col, (n, LANE)))[0:1, :]
    decay = jnp.exp(jnp.minimum(g_col - g_row, 0.0))

    row = _iota((n, n), 0)
    col = _iota((n, n), 1)
    same = (row >> _log2(L)) == (col >> _log2(L))
    qk_k = _mm(jnp.concatenate([k_s, q_s], axis=0), k_s, 1, 1, P)
    a_mat = jnp.where(same & (col < row), beta_col * qk_k[0:n] * decay, 0.0)
    attn = jnp.where(same & (col <= row), qk_k[n:2 * n] * decay, 0.0)
    t_inv = _unit_lower_inverse(a_mat, L, P)
    rhs = jnp.concatenate([_stack(beta * eg * k, masks), _stack(beta * v, masks)], axis=1)
    wu = _mm(t_inv, rhs, passes=P)
    s = s_ref[...]
    v_new = wu[:, GROUP_W:2 * GROUP_W] - _mm(wu[:, 0:GROUP_W], s, passes=P)
    o = _mm(q * eg, s, passes=P) + _unstack(_mm(attn, v_new, passes=P), masks, L)
    g_last = gc[L - 1:L]
    s_ref[...] = s * jnp.exp(g_last) + _mm(_stack(k * jnp.exp(g_last - gc), masks), v_new, 0, 0, P)

    o = o * lax.rsqrt(_mm_xc(o * o, bones) * inv_w + NORM_EPS) * nw_ref[...] * _silu(z)
    o_ref[...] = o[0:valid]


def _gla_body(p_ref, s0_ref, g2_ref, gb_ref, nw_ref, o_ref, s_ref, q_scr, k_scr, b_scr, v_scr, *, chunk, valid):
    @pl.when(pl.program_id(1) == 0)
    def _():
        s_ref[...] = s0_ref[...]

    P = MIX_PASSES
    L = chunk
    p = p_ref[...]
    q = p[:, 0:GLA_QK_W] * (GLA_DK ** -0.5)
    k = p[:, GLA_QK_W:2 * GLA_QK_W]
    v = p[:, 2 * GLA_QK_W:2 * GLA_QK_W + GROUP_W]
    gt = p[:, 2 * GLA_QK_W + GROUP_W:2 * GLA_QK_W + 2 * GROUP_W]
    gd = p[:, 2 * GLA_QK_W + 2 * GROUP_W:GLA_P_W]
    gk = -_softplus(-(_mm(gd, g2_ref[...]) + gb_ref[...])) * (1.0 / GLA_TAU)
    q, k, v, gt, gk = (_pad_rows(t, L) for t in (q, k, v, gt, gk))
    if valid < L:
        live = _iota((L, GLA_QK_W), 0) < valid
        gk = jnp.where(live, gk, 0.0)
        k = jnp.where(live, k, 0.0)
    bc = _mm_cx(_tri_incl(L), gk)
    s = s_ref[...]
    q_scr[...] = q
    k_scr[...] = k
    b_scr[...] = bc
    v_scr[...] = v
    spread = _block_ones(GLA_QK_W, GROUP_W, GLA_DK, HEAD_DIM)
    t_idx = _iota((L, GROUP_W), 0)

    def column(j, o):
        kj = k_scr[pl.ds(j, 1), :]
        bj = b_scr[pl.ds(j, 1), :]
        vj = v_scr[pl.ds(j, 1), :]
        w = q_scr[...] * kj * jnp.exp(jnp.minimum(b_scr[...] - bj, 0.0))
        return o + jnp.where(t_idx >= j, _mm(w, spread, passes=P), 0.0) * vj

    o = lax.fori_loop(0, valid, column, _mm(q * jnp.exp(bc), s, 1, 1, P))
    b_last = bc[L - 1:L]
    blockdiag = _block_ones(GROUP_W, GLA_QK_W, HEAD_DIM, GLA_DK).astype(F32)
    s_ref[...] = s * jnp.exp(b_last) + blockdiag * _mm(v, k * jnp.exp(b_last - bc), 0, 0, P)

    bones = _block_ones(GROUP_W, GROUP_W, HEAD_DIM, HEAD_DIM)
    o = o * lax.rsqrt(_mm_xc(o * o, bones) * (1.0 / HEAD_DIM) + NORM_EPS) * nw_ref[...] * _silu(gt)
    o_ref[...] = o[0:valid]


def _rglru_body(p_ref, cv0_ref, h0_ref, cw_ref, cb_ref, wa_ref, ba_ref, wx_ref, bx_ref, lam_ref,
                o_ref, cv_ref, h_ref):
    @pl.when(pl.program_id(1) == 0)
    def _():
        h_ref[...] = h0_ref[...]
        cv_ref[...] = cv0_ref[...]

    p = p_ref[...]
    rows = p.shape[0]
    xb = p[:, 0:GROUP_W]
    gate = p[:, GROUP_W:RG_P_W]
    ext = cv_ref[...]
    cw = cw_ref[...]
    xc = cw[CONV_K - 1:CONV_K] * xb
    for d in range(1, CONV_K):
        xc = xc + cw[CONV_K - 1 - d:CONV_K - d] * _shift_rows(xb, ext, d)
    cv_ref[...] = xb[rows - CARRY_ROWS:rows]
    xc = xc + cb_ref[...]
    r = _sigmoid(_mm(xc, wa_ref[...]) + ba_ref[...])
    i = _sigmoid(_mm(xc, wx_ref[...]) + bx_ref[...])
    log_a = -RG_C * r * _softplus(-lam_ref[...])
    a = jnp.exp(log_a)
    b = jnp.sqrt(1.0 - jnp.exp(2.0 * log_a)) * (i * xc)
    t_idx = _iota((rows, GROUP_W), 0)
    d = 1
    while d < rows:
        keep = t_idx >= d
        a_sh = jnp.where(keep, pltpu.roll(a, d, 0), 1.0)
        b_sh = jnp.where(keep, pltpu.roll(b, d, 0), 0.0)
        b = a * b_sh + b
        a = a * a_sh
        d *= 2
    h = a * h_ref[...] + b
    h_ref[...] = h[rows - 1:rows]
    o_ref[...] = h * (0.5 * gate * (1.0 + jnp.tanh(0.7978845608028654 * (gate + 0.044715 * gate * gate * gate))))


def _mixer_params():
    return pltpu.CompilerParams(dimension_semantics=("parallel", "arbitrary"), vmem_limit_bytes=VMEM_LIMIT)


def _seq_spec(rows, width):
    return pl.BlockSpec((None, rows, width), lambda b, c: (b, c, 0))


def _state_spec(rows, width):
    return pl.BlockSpec((None, rows, width), lambda b, c: (b, 0, 0))


def _param_spec(shape):
    return pl.BlockSpec(shape, lambda b, c: (0,) * len(shape))


def _chunking(T):
    if T % PROMPT_CHUNK == 0:
        return PROMPT_CHUNK, PROMPT_CHUNK
    assert T % CARRY_ROWS == 0 and T <= PAD_CHUNK
    return T, PAD_CHUNK


def _rwkv(p, shift0, s0, lp):
    B, T, _ = p.shape
    valid, chunk = _chunking(T)
    row = lambda name: _param_spec((1, lp[name].shape[-1]))
    return pl.pallas_call(
        functools.partial(_rwkv_body, chunk=chunk, valid=valid),
        grid=(B, T // valid),
        in_specs=[_seq_spec(valid, RWKV_COLS), _state_spec(CARRY_ROWS, RWKV_COLS), _state_spec(GROUP_W, GROUP_W),
                  row("rwkv_mu"), row("rwkv_w0"), _param_spec((LANE, GROUP_W)), row("rwkv_a0"),
                  _param_spec((LANE, GROUP_W)), _param_spec((LANE, GROUP_W)), row("rwkv_k_k"), row("rwkv_k_a"),
                  row("rwkv_r_k"), row("rwkv_ln_w"), row("rwkv_ln_b")],
        out_specs=[_seq_spec(valid, GROUP_W), _state_spec(CARRY_ROWS, RWKV_COLS), _state_spec(GROUP_W, GROUP_W)],
        out_shape=[jax.ShapeDtypeStruct((B, T, GROUP_W), F32), jax.ShapeDtypeStruct((B, CARRY_ROWS, RWKV_COLS), F32),
                   jax.ShapeDtypeStruct((B, GROUP_W, GROUP_W), F32)],
        compiler_params=_mixer_params(),
    )(p, shift0, s0, lp["rwkv_mu"], lp["rwkv_w0"], lp["rwkv_w2"], lp["rwkv_a0"], lp["rwkv_a2"], lp["rwkv_g2"],
      lp["rwkv_k_k"], lp["rwkv_k_a"], lp["rwkv_r_k"], lp["rwkv_ln_w"], lp["rwkv_ln_b"])


def _gdn(p, conv0, s0, lp):
    B, T, _ = p.shape
    valid, chunk = _chunking(T)
    return pl.pallas_call(
        functools.partial(_gdn_body, chunk=chunk, valid=valid),
        grid=(B, T // valid),
        in_specs=[_seq_spec(valid, GDN_P_W), _state_spec(CARRY_ROWS, 3 * GROUP_W), _state_spec(GROUP_W, GROUP_W),
                  _param_spec((CONV_K, 3 * GROUP_W)), _param_spec((1, GROUP_W)), _param_spec((1, GROUP_W)),
                  _param_spec((1, GROUP_W))],
        out_specs=[_seq_spec(valid, GROUP_W), _state_spec(CARRY_ROWS, 3 * GROUP_W), _state_spec(GROUP_W, GROUP_W)],
        out_shape=[jax.ShapeDtypeStruct((B, T, GROUP_W), F32),
                   jax.ShapeDtypeStruct((B, CARRY_ROWS, 3 * GROUP_W), F32),
                   jax.ShapeDtypeStruct((B, GROUP_W, GROUP_W), F32)],
        compiler_params=_mixer_params(),
    )(p, conv0, s0, lp["gdn_conv_w"], lp["gdn_A_log"], lp["gdn_dt_bias"], lp["gdn_norm_w"])


def _gla(p, s0, lp):
    B, T, _ = p.shape
    valid, chunk = _chunking(T)
    return pl.pallas_call(
        functools.partial(_gla_body, chunk=chunk, valid=valid),
        grid=(B, T // valid),
        in_specs=[_seq_spec(valid, GLA_P_W), _state_spec(GROUP_W, GLA_QK_W),
                  _param_spec((LANE, GLA_QK_W)), _param_spec((1, GLA_QK_W)), _param_spec((1, GROUP_W))],
        out_specs=[_seq_spec(valid, GROUP_W), _state_spec(GROUP_W, GLA_QK_W)],
        out_shape=[jax.ShapeDtypeStruct((B, T, GROUP_W), F32), jax.ShapeDtypeStruct((B, GROUP_W, GLA_QK_W), F32)],
        scratch_shapes=[pltpu.VMEM((chunk, GLA_QK_W), F32), pltpu.VMEM((chunk, GLA_QK_W), F32),
                        pltpu.VMEM((chunk, GLA_QK_W), F32), pltpu.VMEM((chunk, GROUP_W), F32)],
        compiler_params=_mixer_params(),
    )(p, s0, lp["gla_g2"], lp["gla_gb"], lp["gla_norm_w"])


def _rglru(p, conv0, h0, lp):
    B, T, _ = p.shape
    rows = RG_CHUNK if T % RG_CHUNK == 0 else T
    return pl.pallas_call(
        _rglru_body,
        grid=(B, T // rows),
        in_specs=[_seq_spec(rows, RG_P_W), _state_spec(CARRY_ROWS, GROUP_W), _state_spec(1, GROUP_W),
                  _param_spec((CONV_K, GROUP_W)), _param_spec((1, GROUP_W)),
                  _param_spec((GROUP_W, GROUP_W)), _param_spec((1, GROUP_W)),
                  _param_spec((GROUP_W, GROUP_W)), _param_spec((1, GROUP_W)), _param_spec((1, GROUP_W))],
        out_specs=[_seq_spec(rows, GROUP_W), _state_spec(CARRY_ROWS, GROUP_W), _state_spec(1, GROUP_W)],
        out_shape=[jax.ShapeDtypeStruct((B, T, GROUP_W), F32), jax.ShapeDtypeStruct((B, CARRY_ROWS, GROUP_W), F32),
                   jax.ShapeDtypeStruct((B, 1, GROUP_W), F32)],
        compiler_params=_mixer_params(),
    )(p, conv0, h0, lp["rg_conv_w"], lp["rg_conv_b"], lp["rg_wa"], lp["rg_ba"], lp["rg_wx"], lp["rg_bx"],
      lp["rg_lambda"])


def _block_diag(blocks):
    H, r, c = blocks.shape[-3:]
    eye = jnp.eye(H, dtype=blocks.dtype)
    out = blocks[..., :, :, None, :] * eye[:, None, :, None]
    return out.reshape(*blocks.shape[:-3], H * r, H * c)


def _diag_blocks(mat, H):
    r, c = mat.shape[-2] // H, mat.shape[-1] // H
    m = mat.reshape(*mat.shape[:-2], H, r, H, c)
    return jnp.stack([m[..., h, :, h, :] for h in range(H)], axis=-3)


def _pad_cols(w, offset, width):
    return jnp.pad(w, ((0, 0), (offset, width - offset - w.shape[1])))


def _pad_rows_to_lane(w, offset):
    return jnp.pad(w, ((offset, LANE - offset - w.shape[0]), (0, 0)))


def _layer_params(raw, i):
    g = lambda name: raw[name][i]
    row = lambda name: g(name).reshape(1, -1)
    w_in = g("w_in")
    o_rw, o_rg, o_gdn = RWKV_COLS, RWKV_COLS + 2 * GROUP_W, RWKV_COLS + 2 * GROUP_W + 4 * GROUP_W + 2 * N_HEADS
    gdn = w_in[:, o_rg:o_gdn]
    gla = w_in[:, o_gdn:]
    w_in_p = jnp.concatenate([
        w_in[:, :o_rg],
        gdn[:, :4 * GROUP_W], _pad_cols(gdn[:, 4 * GROUP_W:], 0, LANE),
        gla[:, :2 * GLA_QK_W + 2 * GROUP_W], _pad_cols(gla[:, 2 * GLA_QK_W + 2 * GROUP_W:], 0, LANE)], axis=1)
    lp = {
        "w_in": w_in_p.astype(BF16), "w_out": g("w_out").astype(BF16),
        "rwkv_w2": _pad_rows_to_lane(g("rwkv_w2"), 0),
        "rwkv_a2": _pad_rows_to_lane(g("rwkv_a2"), 32),
        "rwkv_g2": _pad_rows_to_lane(g("rwkv_g2"), 64),
        "rg_wa": _block_diag(g("rg_wa")), "rg_wx": _block_diag(g("rg_wx")),
        "gdn_conv_w": g("gdn_conv_w"), "rg_conv_w": g("rg_conv_w"),
        "gdn_A_log": jnp.repeat(g("gdn_A_log"), HEAD_DIM).reshape(1, -1),
        "gdn_dt_bias": jnp.repeat(g("gdn_dt_bias"), HEAD_DIM).reshape(1, -1),
        "gdn_norm_w": jnp.tile(g("gdn_norm_w"), N_HEADS).reshape(1, -1),
        "gla_norm_w": jnp.tile(g("gla_norm_w"), N_HEADS).reshape(1, -1),
        "gla_g2": _pad_rows_to_lane(g("gla_g2"), 0),
    }
    for name in ("ffn1_gate", "ffn1_up", "ffn1_down", "ffn2_gate", "ffn2_up", "ffn2_down"):
        lp[name] = g(name).astype(BF16)
    for name in ("norm_ffn1", "norm_mix", "norm_ffn2", "rwkv_mu", "rwkv_w0", "rwkv_a0", "rwkv_k_k", "rwkv_k_a",
                 "rwkv_r_k", "rwkv_ln_w", "rwkv_ln_b", "rg_conv_b", "rg_ba", "rg_bx", "rg_lambda", "gla_gb"):
        lp[name] = row(name)
    return lp


def _carry_tile(rows):
    return jnp.pad(rows, ((0, 0), (CARRY_ROWS - rows.shape[1], 0), (0, 0)))


def _run_trunk(x, states, lps, norm_final):
    B, T, _ = x.shape
    rows = B * T
    x2 = x.reshape(rows, D_MODEL)
    new_states = []
    for i, lp in enumerate(lps):
        if states is None:
            z = lambda *s: jnp.zeros((B,) + s, F32)
            sh0, rw0 = z(CARRY_ROWS, RWKV_COLS), z(GROUP_W, GROUP_W)
            rgc0, rgh0 = z(CARRY_ROWS, GROUP_W), z(1, GROUP_W)
            gdc0, gds0 = z(CARRY_ROWS, 3 * GROUP_W), z(GROUP_W, GROUP_W)
            gla0 = z(GROUP_W, GLA_QK_W)
        else:
            rw_shift, rw_S, rg_conv, rg_h, gdn_conv, gdn_S, gla_S = states[i]
            sh0 = _carry_tile(rw_shift[:, None, :])
            rw0 = _block_diag(rw_S)
            rgc0, rgh0 = _carry_tile(rg_conv), rg_h[:, None, :]
            gdc0, gds0 = _carry_tile(gdn_conv), _block_diag(gdn_S)
            gla0 = _block_diag(jnp.swapaxes(gla_S, -1, -2))
        x2, p_rw, p_rg, p_gdn, p_gla = _ffn_in(x2, lp)
        seq = lambda t: t.reshape(B, T, t.shape[-1])
        o_rw, sh1, rw1 = _rwkv(seq(p_rw), sh0, rw0, lp)
        o_rg, rgc1, rgh1 = _rglru(seq(p_rg), rgc0, rgh0, lp)
        o_gdn, gdc1, gds1 = _gdn(seq(p_gdn), gdc0, gds0, lp)
        o_gla, gla1 = _gla(seq(p_gla), gla0, lp)
        outs = [t.reshape(rows, GROUP_W) for t in (o_rw, o_rg, o_gdn, o_gla)]
        x2 = _out_ffn(x2, outs, lp, norm_final, final_norm=(i == len(lps) - 1))
        tail = CARRY_ROWS - (CONV_K - 1)
        new_states.append((sh1[:, CARRY_ROWS - 1], _diag_blocks(rw1, N_HEADS), rgc1[:, tail:], rgh1[:, 0],
                           gdc1[:, tail:], _diag_blocks(gds1, N_HEADS),
                           jnp.swapaxes(_diag_blocks(gla1, N_HEADS), -1, -2)))
    stacked = tuple(jnp.stack([st[j] for st in new_states]) for j in range(len(new_states[0])))
    return x2.reshape(B, T, D_MODEL), stacked


_STACKED = ("norm_ffn1", "ffn1_gate", "ffn1_up", "ffn1_down", "norm_mix", "w_in", "w_out", "rwkv_mu", "rwkv_w0",
            "rwkv_w2", "rwkv_a0", "rwkv_a2", "rwkv_g2", "rwkv_k_k", "rwkv_k_a", "rwkv_r_k", "rwkv_ln_w",
            "rwkv_ln_b", "rg_conv_w", "rg_conv_b", "rg_wa", "rg_ba", "rg_wx", "rg_bx", "rg_lambda", "gdn_conv_w",
            "gdn_A_log", "gdn_dt_bias", "gdn_norm_w", "gla_g2", "gla_gb", "gla_norm_w", "norm_ffn2", "ffn2_gate",
            "ffn2_up", "ffn2_down")


def kernel(x_prompt, x_sample, state_rwkv_shift, state_rwkv_wkv, state_rglru_conv, state_rglru_h, state_gdn_conv, state_gdn_S, state_gla_S, norm_ffn1, ffn1_gate, ffn1_up, ffn1_down, norm_mix, w_in, w_out, rwkv_mu, rwkv_w0, rwkv_w2, rwkv_a0, rwkv_a2, rwkv_g2, rwkv_k_k, rwkv_k_a, rwkv_r_k, rwkv_ln_w, rwkv_ln_b, rg_conv_w, rg_conv_b, rg_wa, rg_ba, rg_wx, rg_bx, rg_lambda, gdn_conv_w, gdn_A_log, gdn_dt_bias, gdn_norm_w, gla_g2, gla_gb, gla_norm_w, norm_ffn2, ffn2_gate, ffn2_up, ffn2_down, norm_final):
    values = (norm_ffn1, ffn1_gate, ffn1_up, ffn1_down, norm_mix, w_in, w_out, rwkv_mu, rwkv_w0, rwkv_w2, rwkv_a0,
              rwkv_a2, rwkv_g2, rwkv_k_k, rwkv_k_a, rwkv_r_k, rwkv_ln_w, rwkv_ln_b, rg_conv_w, rg_conv_b, rg_wa,
              rg_ba, rg_wx, rg_bx, rg_lambda, gdn_conv_w, gdn_A_log, gdn_dt_bias, gdn_norm_w, gla_g2, gla_gb,
              gla_norm_w, norm_ffn2, ffn2_gate, ffn2_up, ffn2_down)
    raw = dict(zip(_STACKED, values))
    depth = w_in.shape[0]
    lps = [_layer_params(raw, i) for i in range(depth)]
    nf = norm_final.reshape(1, -1)

    y_prompt, new_p = _run_trunk(x_prompt, None, lps, nf)
    sample_in = (state_rwkv_shift, state_rwkv_wkv, state_rglru_conv, state_rglru_h,
                 state_gdn_conv, state_gdn_S, state_gla_S)
    y_sample, new_s = _run_trunk(x_sample, [tuple(s[i] for s in sample_in) for i in range(depth)], lps, nf)
    return (y_prompt, y_sample) + new_p + new_s
```

```python
import functools

import jax
import jax.numpy as jnp
from jax import lax
from jax.experimental import pallas as pl
from jax.experimental.pallas import tpu as pltpu

F32 = jnp.float32
BF16 = jnp.bfloat16

D_MODEL = 1024
GROUP_W = 256
HEAD_DIM = 64
N_HEADS = 4
GLA_DK = 32
GLA_QK_W = N_HEADS * GLA_DK
RWKV_COLS = 896
CONV_K = 4
D_FF = 2816
NORM_EPS = 1e-6
RWKV_GN_EPS = 64e-5
RG_C = 8.0
GLA_TAU = 16.0
LANE = 128
SUBLANE = 8
CARRY_ROWS = SUBLANE
VMEM_LIMIT = 56 * 1024 * 1024
MIXER_VMEM_LIMIT = 32 * 1024 * 1024

GDN_P_W = 3 * GROUP_W + GROUP_W + LANE
GLA_P_W = 2 * GLA_QK_W + 2 * GROUP_W + LANE
RG_P_W = 2 * GROUP_W

FFN_TM = 512
FFN_TF = 1408
PROMPT_CHUNK = 64
SYSTEM_ROWS = 64
MAX_STATE_SEQS = 16
GLA_CHUNK = 16
GLA_ROWS = 128
RG_CHUNK = 256
MIX_PASSES = 1


def _dg(a, b, ca, cb):
    return lax.dot_general(a, b, (((ca,), (cb,)), ((), ())), preferred_element_type=F32)


def _split2(x):
    hi = x.astype(BF16)
    lo = (x - hi.astype(F32)).astype(BF16)
    return hi, lo


def _split3(x):
    hi = x.astype(BF16)
    r1 = x - hi.astype(F32)
    mid = r1.astype(BF16)
    lo = (r1 - mid.astype(F32)).astype(BF16)
    return hi, mid, lo


def _mm(a, b, ca=1, cb=0, passes=1):
    if passes == 1:
        return _dg(a.astype(BF16), b.astype(BF16), ca, cb)
    a1, a2 = _split2(a)
    b1, b2 = _split2(b)
    return _dg(a1, b1, ca, cb) + (_dg(a1, b2, ca, cb) + _dg(a2, b1, ca, cb))


def _mm_xc(x, c, ca=1, cb=0):
    x1, x2, x3 = _split3(x)
    return _dg(x1, c, ca, cb) + (_dg(x2, c, ca, cb) + _dg(x3, c, ca, cb))


def _mm_cx(c, x, ca=1, cb=0):
    x1, x2, x3 = _split3(x)
    return _dg(c, x1, ca, cb) + (_dg(c, x2, ca, cb) + _dg(c, x3, ca, cb))


def _iota(shape, dim):
    return lax.broadcasted_iota(jnp.int32, shape, dim)


def _log2(n):
    assert n & (n - 1) == 0
    return n.bit_length() - 1


def _head_masks(width, head_w):
    lane = _iota((1, width), 1)
    return [((lane >> _log2(head_w)) == h).astype(F32) for h in range(width // head_w)]


def _stack(x, masks):
    return jnp.concatenate([x * m for m in masks], axis=0)


def _unstack(xs, masks, rows):
    out = xs[0:rows] * masks[0]
    for h in range(1, len(masks)):
        out = out + xs[h * rows:(h + 1) * rows] * masks[h]
    return out


def _block_ones(rows, cols, rblk, cblk):
    r = _iota((rows, cols), 0) >> _log2(rblk)
    c = _iota((rows, cols), 1) >> _log2(cblk)
    return (r == c).astype(F32).astype(BF16)


def _tri_incl(rows, chunk):
    row = _iota((rows, rows), 0)
    col = _iota((rows, rows), 1)
    same = (row >> _log2(chunk)) == (col >> _log2(chunk))
    return (same & (col <= row)).astype(F32).astype(BF16)


def _softplus(x):
    return jnp.maximum(x, 0.0) + jnp.log1p(jnp.exp(-jnp.abs(x)))


def _sigmoid(x):
    return 1.0 / (1.0 + jnp.exp(-x))


def _silu(x):
    return x * _sigmoid(x)


def _rmsnorm(x, w):
    return x * lax.rsqrt(jnp.mean(x * x, axis=-1, keepdims=True) + NORM_EPS) * w


def _unit_lower_inverse(a, chunk, passes):
    n = a.shape[0]
    row = _iota((n, n), 0)
    col = _iota((n, n), 1)
    eye = (row == col).astype(F32)
    base = _log2(SUBLANE)
    a0 = jnp.where((row >> base) == (col >> base), a, 0.0)
    a2 = _mm(a0, a0, passes=passes)
    yield
    a4 = _mm(a2, a2, passes=passes)
    t = _mm(eye - a0, eye + a2, passes=passes)
    yield
    t = _mm(t, eye + a4, passes=passes)
    yield
    size = SUBLANE
    while size < chunk:
        lo = _log2(size)
        off = jnp.where(((row >> (lo + 1)) == (col >> (lo + 1))) & ((row >> lo) != (col >> lo)), a, 0.0)
        t_off = _mm(t, off, passes=passes)
        yield
        t = t - _mm(t_off, t, passes=passes)
        yield
        size *= 2
    return t


def _lockstep(generators):
    results = [None] * len(generators)
    live = list(enumerate(generators))
    while live:
        still = []
        for i, gen in live:
            try:
                next(gen)
                still.append((i, gen))
            except StopIteration as stop:
                results[i] = stop.value
        live = still
    return results


def _swiglu_half_step(x, nw, wg_ref, wu_ref, wd_ref):
    hb = _rmsnorm(x, nw).astype(BF16)
    acc = jnp.zeros_like(x)
    for c in range(D_FF // FFN_TF):
        sl = slice(c * FFN_TF, (c + 1) * FFN_TF)
        g = _dg(hb, wg_ref[:, sl], 1, 0)
        u = _dg(hb, wu_ref[:, sl], 1, 0)
        acc = acc + _dg((_silu(g) * u).astype(BF16), wd_ref[sl, :], 1, 0)
    return x + 0.5 * acc


def _ffn_in_body(x_ref, n1_ref, wg_ref, wu_ref, wd_ref, nm_ref, win_ref,
                 x_out, prw_out, prg_out, pgdn_out, pgla_out):
    x1 = _swiglu_half_step(x_ref[...], n1_ref[...], wg_ref, wu_ref, wd_ref)
    x_out[...] = x1
    hb = _rmsnorm(x1, nm_ref[...]).astype(BF16)
    off = 0
    for out in (prw_out, prg_out, pgdn_out, pgla_out):
        w = out.shape[-1]
        out[...] = _dg(hb, win_ref[:, off:off + w], 1, 0)
        off += w


def _out_ffn_body(x_ref, orw_ref, org_ref, ogdn_ref, ogla_ref, wo_ref, n2_ref, wg_ref, wu_ref, wd_ref,
                  nf_ref, y_out, *, final_norm):
    mixed = jnp.concatenate([orw_ref[...], org_ref[...], ogdn_ref[...], ogla_ref[...]], axis=-1)
    x2 = x_ref[...] + _dg(mixed.astype(BF16), wo_ref[...], 1, 0)
    y = _swiglu_half_step(x2, n2_ref[...], wg_ref, wu_ref, wd_ref)
    if final_norm:
        y = _rmsnorm(y, nf_ref[...])
    y_out[...] = y


def _resident(shape):
    return pl.BlockSpec(shape, lambda *_: (0,) * len(shape), pipeline_mode=pl.Buffered(1))


def _row_spec(width):
    return pl.BlockSpec((FFN_TM, width), lambda i: (i, 0))


def _dense_params():
    return pltpu.CompilerParams(dimension_semantics=("parallel",), vmem_limit_bytes=VMEM_LIMIT)


def _ffn_in(x, lp):
    rows = x.shape[0]
    widths = (RWKV_COLS, RG_P_W, GDN_P_W, GLA_P_W)
    return pl.pallas_call(
        _ffn_in_body,
        name="ffn_in",
        grid=(rows // FFN_TM,),
        in_specs=[_row_spec(D_MODEL), _resident((1, D_MODEL)),
                  _resident((D_MODEL, D_FF)), _resident((D_MODEL, D_FF)), _resident((D_FF, D_MODEL)),
                  _resident((1, D_MODEL)), _resident((D_MODEL, sum(widths)))],
        out_specs=[_row_spec(D_MODEL)] + [_row_spec(w) for w in widths],
        out_shape=[jax.ShapeDtypeStruct((rows, D_MODEL), F32)]
        + [jax.ShapeDtypeStruct((rows, w), F32) for w in widths],
        compiler_params=_dense_params(),
    )(x, lp["norm_ffn1"], lp["ffn1_gate"], lp["ffn1_up"], lp["ffn1_down"], lp["norm_mix"], lp["w_in"])


def _out_ffn(x, outs, lp, norm_final, final_norm):
    rows = x.shape[0]
    return pl.pallas_call(
        functools.partial(_out_ffn_body, final_norm=final_norm),
        name="out_ffn",
        grid=(rows // FFN_TM,),
        in_specs=[_row_spec(D_MODEL)] + [_row_spec(GROUP_W)] * 4
        + [_resident((D_MODEL, D_MODEL)), _resident((1, D_MODEL)),
           _resident((D_MODEL, D_FF)), _resident((D_MODEL, D_FF)), _resident((D_FF, D_MODEL)),
           _resident((1, D_MODEL))],
        out_specs=_row_spec(D_MODEL),
        out_shape=jax.ShapeDtypeStruct((rows, D_MODEL), F32),
        compiler_params=_dense_params(),
    )(x, *outs, lp["w_out"], lp["norm_ffn2"], lp["ffn2_gate"], lp["ffn2_up"], lp["ffn2_down"], norm_final)


def _seq_rows(ref, g0, merge):
    x3 = ref[g0:g0 + merge]
    return x3.reshape(x3.shape[0] * x3.shape[1], x3.shape[2])


def _carry_rows(ref, g0, merge, chunk):
    ext3 = ref[g0:g0 + merge]
    if chunk > CARRY_ROWS:
        ext3 = jnp.concatenate([ext3, jnp.zeros((merge, chunk - CARRY_ROWS, ext3.shape[2]), F32)], axis=1)
    return ext3.reshape(merge * chunk, ext3.shape[2])


def _shift_seq(x, ext, d, t_in):
    rows = x.shape[0]
    return jnp.where(t_in < d, pltpu.roll(ext, (rows - CARRY_ROWS + d) % rows, 0), pltpu.roll(x, d, 0))


def _causal_conv(x, ext, cw, t_in):
    y = cw[CONV_K - 1:CONV_K] * x
    for d in range(1, CONV_K):
        y = y + cw[CONV_K - 1 - d:CONV_K - d] * _shift_seq(x, ext, d, t_in)
    return y


def _store_seq(ref, g0, merge, chunk, x):
    ref[g0:g0 + merge] = x.reshape(merge, chunk, x.shape[1])


def _store_carry(ref, g0, merge, chunk, x):
    ref[g0:g0 + merge] = x.reshape(merge, chunk, x.shape[1])[:, chunk - CARRY_ROWS:chunk, :]


def _block_masks(n, chunk):
    row = _iota((n, n), 0)
    col = _iota((n, n), 1)
    same = (row >> _log2(chunk)) == (col >> _log2(chunk))
    return same & (col < row), same & (col <= row)


def _rwkv_body(p_ref, sh0_ref, s0_ref, mu_ref, w0_ref, w2_ref, a0_ref, a2_ref, g2_ref, kk_ref, ka_ref,
               rk_ref, lnw_ref, lnb_ref, o_ref, sh_ref, s_ref, *, chunk, merge, systems):
    @pl.when(pl.program_id(1) == 0)
    def _():
        s_ref[...] = s0_ref[...]
        sh_ref[...] = sh0_ref[...]

    P = MIX_PASSES
    L = chunk
    R = merge * L
    n = N_HEADS * R
    bones = _block_ones(GROUP_W, GROUP_W, HEAD_DIM, HEAD_DIM)
    blockdiag = bones.astype(F32)
    masks = _head_masks(GROUP_W, HEAD_DIM)
    strict, incl = _block_masks(n, L)
    tri = _tri_incl(R, L)
    t_in = _iota((R, RWKV_COLS), 0) & (L - 1)

    def system(p, ext, states):
        prev = _shift_seq(p, ext, 1, t_in)
        xm = p + (prev - p) * mu_ref[...]
        r = xm[:, 0:GROUP_W]
        k = xm[:, GROUP_W:2 * GROUP_W]
        v = xm[:, 2 * GROUP_W:3 * GROUP_W]
        x_lr = xm[:, 3 * GROUP_W:RWKV_COLS]
        w_log = -_softplus(-(w0_ref[...] + _mm(jnp.tanh(x_lr), w2_ref[...]))) - 0.5
        lw = -jnp.exp(w_log)
        a_icl = _sigmoid(a0_ref[...] + _mm(x_lr, a2_ref[...]))
        gate = _mm(_sigmoid(x_lr), g2_ref[...])
        kk = k * kk_ref[...]
        kk = kk * lax.rsqrt(_mm_xc(kk * kk, bones) + 1e-12)
        k = k * (1.0 + (a_icl - 1.0) * ka_ref[...])
        b_vec = kk * a_icl
        yield

        lg = _mm_cx(tri, lw)
        dec_out = jnp.exp(-lg)
        a_x = -kk * jnp.exp(lg - lw)
        r_x = r * jnp.exp(lg)
        a_s = _stack(a_x, masks)
        r_s = _stack(r_x, masks)
        v_s = _stack(v, masks)
        bk_s = jnp.concatenate([_stack(b_vec * dec_out, masks), _stack(k * dec_out, masks)], axis=0)
        yield
        ar_bk = _mm(jnp.concatenate([a_s, r_s], axis=0), bk_s, 1, 1, P)
        a_ab = jnp.where(strict, ar_bk[0:n, 0:n], 0.0)
        a_ak = jnp.where(strict, ar_bk[0:n, n:2 * n], 0.0)
        m_rb = jnp.where(incl, ar_bk[n:2 * n, 0:n], 0.0)
        m_rk = jnp.where(incl, ar_bk[n:2 * n, n:2 * n], 0.0)

        za, zr = [], []
        for g, s in enumerate(states):
            rows = slice(g * L, (g + 1) * L)
            za.append(_mm(a_x[rows], s, 1, 1, P))
            zr.append(_mm(r_x[rows], s, 1, 1, P))
        za = za[0] if merge == 1 else jnp.concatenate(za, axis=0)
        zr = zr[0] if merge == 1 else jnp.concatenate(zr, axis=0)
        rhs_u = _stack(za, masks) + _mm(a_ak, v_s, passes=P)
        yield
        t_inv = yield from _unit_lower_inverse(-a_ab, L, P)
        u = _mm(t_inv, rhs_u, passes=P)
        yield
        uv = jnp.concatenate([u, v_s], axis=0)
        y = zr + _unstack(_mm(jnp.concatenate([m_rb, m_rk], axis=1), uv, passes=P), masks, R)
        u_x = _unstack(u, masks, R)
        yield
        new_states = []
        for g, s in enumerate(states):
            rows = slice(g * L, (g + 1) * L)
            lg_g = lg[rows]
            lg_last = lg_g[L - 1:L]
            carry = jnp.exp(lg_last - lg_g)
            lhs = jnp.concatenate([u_x[rows], v[rows]], axis=0)
            rhs = jnp.concatenate([b_vec[rows] * carry, k[rows] * carry], axis=0)
            new_states.append(s * jnp.exp(lg_last) + blockdiag * _mm(lhs, rhs, 0, 0, P))

        inv_n = 1.0 / HEAD_DIM
        mean = _mm_xc(y, bones) * inv_n
        yield
        yc = y - mean
        var = _mm_xc(yc * yc, bones) * inv_n
        yield
        yn = yc * lax.rsqrt(var + RWKV_GN_EPS) * lnw_ref[...] + lnb_ref[...]
        bonus = _mm_xc(r * k * rk_ref[...], bones) * v
        return (yn + bonus) * gate, new_states

    loaded = [(_seq_rows(p_ref, i * merge, merge), _carry_rows(sh_ref, i * merge, merge, L),
               [s_ref[i * merge + g] for g in range(merge)]) for i in range(systems)]
    results = _lockstep([system(*args) for args in loaded])
    for i, ((p, _, _), (out, new_states)) in enumerate(zip(loaded, results)):
        _store_carry(sh_ref, i * merge, merge, L, p)
        _store_seq(o_ref, i * merge, merge, L, out)
        for g, s in enumerate(new_states):
            s_ref[i * merge + g] = s


def _gdn_body(p_ref, cv0_ref, s0_ref, cw_ref, alog_ref, dtb_ref, nw_ref, o_ref, cv_ref, s_ref, *,
              chunk, merge, systems):
    @pl.when(pl.program_id(1) == 0)
    def _():
        s_ref[...] = s0_ref[...]
        cv_ref[...] = cv0_ref[...]

    P = MIX_PASSES
    L = chunk
    R = merge * L
    n = N_HEADS * R
    bones = _block_ones(GROUP_W, GROUP_W, HEAD_DIM, HEAD_DIM)
    blockdiag = bones.astype(F32)
    masks = _head_masks(GROUP_W, HEAD_DIM)
    strict, incl = _block_masks(n, L)
    tri = _tri_incl(R, L)
    t_in = _iota((R, 3 * GROUP_W), 0) & (L - 1)
    src = _iota((LANE, GROUP_W), 0)
    dst = _iota((LANE, GROUP_W), 1) >> _log2(HEAD_DIM)
    spread_b = (src == dst).astype(F32).astype(BF16)
    spread_a = (src == dst + N_HEADS).astype(F32).astype(BF16)
    inv_w = 1.0 / HEAD_DIM

    def system(p, ext, states):
        qkv_in = p[:, 0:3 * GROUP_W]
        z = p[:, 3 * GROUP_W:4 * GROUP_W]
        ba = p[:, 4 * GROUP_W:GDN_P_W]
        qkv = _silu(_causal_conv(qkv_in, ext, cw_ref[...], t_in))
        q = qkv[:, 0:GROUP_W]
        k = qkv[:, GROUP_W:2 * GROUP_W]
        v = qkv[:, 2 * GROUP_W:3 * GROUP_W]
        q = q * lax.rsqrt(_mm_xc(q * q, bones) + 1e-12) * (HEAD_DIM ** -0.5)
        k = k * lax.rsqrt(_mm_xc(k * k, bones) + 1e-12)
        beta = _sigmoid(_mm_xc(ba, spread_b))
        g = -jnp.exp(alog_ref[...]) * _softplus(_mm_xc(ba, spread_a) + dtb_ref[...])
        yield

        gc = _mm_cx(tri, g)
        yield
        eg = jnp.exp(gc)
        k_s = _stack(k, masks)
        g_col = jnp.sum(_stack(gc, masks), axis=-1, keepdims=True) * inv_w
        beta_col = jnp.sum(_stack(beta, masks), axis=-1, keepdims=True) * inv_w
        g_row = jnp.transpose(jnp.broadcast_to(g_col, (n, LANE)))[0:1, :]
        decay = jnp.exp(jnp.minimum(g_col - g_row, 0.0))
        kq_k = _mm(jnp.concatenate([k_s, _stack(q, masks)], axis=0), k_s, 1, 1, P)
        yield
        a_mat = jnp.where(strict, beta_col * kq_k[0:n] * decay, 0.0)
        attn = jnp.where(incl, kq_k[n:2 * n] * decay, 0.0)
        rhs = jnp.concatenate([_stack(beta * eg * k, masks), _stack(beta * v, masks)], axis=1)
        t_inv = yield from _unit_lower_inverse(a_mat, L, P)
        wu = _mm(t_inv, rhs, passes=P)
        yield
        w_x = _unstack(wu[:, 0:GROUP_W], masks, R)
        qe = q * eg
        ws, qs = [], []
        for gi, s in enumerate(states):
            rows = slice(gi * L, (gi + 1) * L)
            ws.append(_mm(w_x[rows], s, passes=P))
            qs.append(_mm(qe[rows], s, passes=P))
        ws = ws[0] if merge == 1 else jnp.concatenate(ws, axis=0)
        qs = qs[0] if merge == 1 else jnp.concatenate(qs, axis=0)
        yield
        v_new = wu[:, GROUP_W:2 * GROUP_W] - _stack(ws, masks)
        o = qs + _unstack(_mm(attn, v_new, passes=P), masks, R)
        vn_x = _unstack(v_new, masks, R)
        yield
        new_states = []
        for gi, s in enumerate(states):
            rows = slice(gi * L, (gi + 1) * L)
            gc_g = gc[rows]
            g_last = gc_g[L - 1:L]
            k_hat = k[rows] * jnp.exp(g_last - gc_g)
            new_states.append(s * jnp.exp(g_last) + blockdiag * _mm(k_hat, vn_x[rows], 0, 0, P))

        o = o * lax.rsqrt(_mm_xc(o * o, bones) * inv_w + NORM_EPS) * nw_ref[...] * _silu(z)
        return o, new_states

    loaded = [(_seq_rows(p_ref, i * merge, merge), _carry_rows(cv_ref, i * merge, merge, L),
               [s_ref[i * merge + g] for g in range(merge)]) for i in range(systems)]
    results = _lockstep([system(*args) for args in loaded])
    for i, ((p, _, _), (out, new_states)) in enumerate(zip(loaded, results)):
        _store_carry(cv_ref, i * merge, merge, L, p[:, 0:3 * GROUP_W])
        _store_seq(o_ref, i * merge, merge, L, out)
        for g, s in enumerate(new_states):
            s_ref[i * merge + g] = s


def _gla_body(p_ref, s0_ref, g2_ref, gb_ref, nw_ref, o_ref, s_ref, *, chunk, merge):
    @pl.when(pl.program_id(1) == 0)
    def _():
        s_ref[...] = s0_ref[...]

    P = MIX_PASSES
    L = chunk
    R = merge * L
    p = _seq_rows(p_ref, 0, merge)
    q = p[:, 0:GLA_QK_W] * (GLA_DK ** -0.5)
    k = p[:, GLA_QK_W:2 * GLA_QK_W]
    v = p[:, 2 * GLA_QK_W:2 * GLA_QK_W + GROUP_W]
    gt = p[:, 2 * GLA_QK_W + GROUP_W:2 * GLA_QK_W + 2 * GROUP_W]
    gd = p[:, 2 * GLA_QK_W + 2 * GROUP_W:GLA_P_W]
    gk = -_softplus(-(_mm(gd, g2_ref[...]) + gb_ref[...])) * (1.0 / GLA_TAU)
    bc = _mm_cx(_tri_incl(R, L), gk)
    spread = _block_ones(GLA_QK_W, GROUP_W, GLA_DK, HEAD_DIM)
    blockdiag = _block_ones(GROUP_W, GLA_QK_W, HEAD_DIM, GLA_DK).astype(F32)
    t_in = _iota((R, GROUP_W), 0) & (L - 1)

    def in_seq_row(x, j):
        x3 = x.reshape(merge, L, x.shape[1])
        return jnp.broadcast_to(x3[:, j:j + 1, :], x3.shape).reshape(x.shape)

    qe = q * jnp.exp(bc)
    o = []
    for g in range(merge):
        o.append(_mm(qe[g * L:(g + 1) * L], s_ref[g], 1, 1, P))
    o = o[0] if merge == 1 else jnp.concatenate(o, axis=0)
    for j in range(L):
        w = q * in_seq_row(k, j) * jnp.exp(jnp.minimum(bc - in_seq_row(bc, j), 0.0))
        o = o + jnp.where(t_in >= j, _mm(w, spread, passes=P), 0.0) * in_seq_row(v, j)
    for g in range(merge):
        rows = slice(g * L, (g + 1) * L)
        bc_g = bc[rows]
        b_last = bc_g[L - 1:L]
        k_hat = k[rows] * jnp.exp(b_last - bc_g)
        s_ref[g] = s_ref[g] * jnp.exp(b_last) + blockdiag * _mm(v[rows], k_hat, 0, 0, P)

    bones = _block_ones(GROUP_W, GROUP_W, HEAD_DIM, HEAD_DIM)
    o = o * lax.rsqrt(_mm_xc(o * o, bones) * (1.0 / HEAD_DIM) + NORM_EPS) * nw_ref[...] * _silu(gt)
    _store_seq(o_ref, 0, merge, L, o)


def _rglru_body(p_ref, cv0_ref, h0_ref, cw_ref, cb_ref, wa_ref, ba_ref, wx_ref, bx_ref, lam_ref,
                o_ref, cv_ref, h_ref, *, chunk, merge):
    @pl.when(pl.program_id(1) == 0)
    def _():
        h_ref[...] = h0_ref[...]
        cv_ref[...] = cv0_ref[...]

    L = chunk
    R = merge * L
    p = _seq_rows(p_ref, 0, merge)
    xb = p[:, 0:GROUP_W]
    gate = p[:, GROUP_W:RG_P_W]
    t_in = _iota((R, GROUP_W), 0) & (L - 1)
    xc = _causal_conv(xb, _carry_rows(cv_ref, 0, merge, L), cw_ref[...], t_in) + cb_ref[...]
    _store_carry(cv_ref, 0, merge, L, xb)
    r = _sigmoid(_mm(xc, wa_ref[...]) + ba_ref[...])
    i = _sigmoid(_mm(xc, wx_ref[...]) + bx_ref[...])
    log_a = -RG_C * r * _softplus(-lam_ref[...])
    a = jnp.exp(log_a)
    b = jnp.sqrt(1.0 - jnp.exp(2.0 * log_a)) * (i * xc)
    d = 1
    while d < L:
        keep = t_in >= d
        a_sh = jnp.where(keep, pltpu.roll(a, d, 0), 1.0)
        b_sh = jnp.where(keep, pltpu.roll(b, d, 0), 0.0)
        b = a * b_sh + b
        a = a * a_sh
        d *= 2
    h3 = a.reshape(merge, L, GROUP_W) * h_ref[...] + b.reshape(merge, L, GROUP_W)
    h_ref[...] = h3[:, L - 1:L, :]
    h = h3.reshape(R, GROUP_W)
    gelu = 0.5 * gate * (1.0 + jnp.tanh(0.7978845608028654 * (gate + 0.044715 * gate * gate * gate)))
    _store_seq(o_ref, 0, merge, L, h * gelu)


def _mixer_params():
    return pltpu.CompilerParams(dimension_semantics=("parallel", "arbitrary"), vmem_limit_bytes=MIXER_VMEM_LIMIT)


def _seq_spec(seqs, rows, width):
    return pl.BlockSpec((seqs, rows, width), lambda b, c: (b, c, 0))


def _state_spec(seqs, rows, width):
    return pl.BlockSpec((seqs, rows, width), lambda b, c: (b, 0, 0))


def _param_spec(shape):
    return pl.BlockSpec(shape, lambda b, c: (0,) * len(shape))


def _delta_cfg(B, T):
    chunk = PROMPT_CHUNK if T % PROMPT_CHUNK == 0 else T
    merge = SYSTEM_ROWS // chunk
    systems = max(s for s in (8, 4, 2, 1) if B % (s * merge) == 0 and s * merge <= MAX_STATE_SEQS)
    return chunk, merge, systems


def _rwkv(p, shift0, s0, lp):
    B, T, _ = p.shape
    chunk, merge, systems = _delta_cfg(B, T)
    seqs = merge * systems
    row = lambda name: _param_spec((1, lp[name].shape[-1]))
    return pl.pallas_call(
        functools.partial(_rwkv_body, chunk=chunk, merge=merge, systems=systems),
        name="rwkv7",
        grid=(B // seqs, T // chunk),
        in_specs=[_seq_spec(seqs, chunk, RWKV_COLS), _state_spec(seqs, CARRY_ROWS, RWKV_COLS),
                  _state_spec(seqs, GROUP_W, GROUP_W),
                  row("rwkv_mu"), row("rwkv_w0"), _param_spec((LANE, GROUP_W)), row("rwkv_a0"),
                  _param_spec((LANE, GROUP_W)), _param_spec((LANE, GROUP_W)), row("rwkv_k_k"), row("rwkv_k_a"),
                  row("rwkv_r_k"), row("rwkv_ln_w"), row("rwkv_ln_b")],
        out_specs=[_seq_spec(seqs, chunk, GROUP_W), _state_spec(seqs, CARRY_ROWS, RWKV_COLS),
                   _state_spec(seqs, GROUP_W, GROUP_W)],
        out_shape=[jax.ShapeDtypeStruct((B, T, GROUP_W), F32), jax.ShapeDtypeStruct((B, CARRY_ROWS, RWKV_COLS), F32),
                   jax.ShapeDtypeStruct((B, GROUP_W, GROUP_W), F32)],
        compiler_params=_mixer_params(),
    )(p, shift0, s0, lp["rwkv_mu"], lp["rwkv_w0"], lp["rwkv_w2"], lp["rwkv_a0"], lp["rwkv_a2"], lp["rwkv_g2"],
      lp["rwkv_k_k"], lp["rwkv_k_a"], lp["rwkv_r_k"], lp["rwkv_ln_w"], lp["rwkv_ln_b"])


def _gdn(p, conv0, s0, lp):
    B, T, _ = p.shape
    chunk, merge, systems = _delta_cfg(B, T)
    seqs = merge * systems
    return pl.pallas_call(
        functools.partial(_gdn_body, chunk=chunk, merge=merge, systems=systems),
        name="gdn",
        grid=(B // seqs, T // chunk),
        in_specs=[_seq_spec(seqs, chunk, GDN_P_W), _state_spec(seqs, CARRY_ROWS, 3 * GROUP_W),
                  _state_spec(seqs, GROUP_W, GROUP_W),
                  _param_spec((CONV_K, 3 * GROUP_W)), _param_spec((1, GROUP_W)), _param_spec((1, GROUP_W)),
                  _param_spec((1, GROUP_W))],
        out_specs=[_seq_spec(seqs, chunk, GROUP_W), _state_spec(seqs, CARRY_ROWS, 3 * GROUP_W),
                   _state_spec(seqs, GROUP_W, GROUP_W)],
        out_shape=[jax.ShapeDtypeStruct((B, T, GROUP_W), F32),
                   jax.ShapeDtypeStruct((B, CARRY_ROWS, 3 * GROUP_W), F32),
                   jax.ShapeDtypeStruct((B, GROUP_W, GROUP_W), F32)],
        compiler_params=_mixer_params(),
    )(p, conv0, s0, lp["gdn_conv_w"], lp["gdn_A_log"], lp["gdn_dt_bias"], lp["gdn_norm_w"])


def _gla(p, s0, lp):
    B, T, _ = p.shape
    chunk = GLA_CHUNK if T % GLA_CHUNK == 0 else T
    merge = max(m for m in (16, 8, 4, 2, 1) if B % m == 0 and m * chunk <= GLA_ROWS)
    return pl.pallas_call(
        functools.partial(_gla_body, chunk=chunk, merge=merge),
        name="gla",
        grid=(B // merge, T // chunk),
        in_specs=[_seq_spec(merge, chunk, GLA_P_W), _state_spec(merge, GROUP_W, GLA_QK_W),
                  _param_spec((LANE, GLA_QK_W)), _param_spec((1, GLA_QK_W)), _param_spec((1, GROUP_W))],
        out_specs=[_seq_spec(merge, chunk, GROUP_W), _state_spec(merge, GROUP_W, GLA_QK_W)],
        out_shape=[jax.ShapeDtypeStruct((B, T, GROUP_W), F32), jax.ShapeDtypeStruct((B, GROUP_W, GLA_QK_W), F32)],
        compiler_params=_mixer_params(),
    )(p, s0, lp["gla_g2"], lp["gla_gb"], lp["gla_norm_w"])


def _rglru(p, conv0, h0, lp):
    B, T, _ = p.shape
    if T % RG_CHUNK == 0:
        chunk, merge = RG_CHUNK, 1
    else:
        assert T == CARRY_ROWS
        chunk, merge = T, B
    return pl.pallas_call(
        functools.partial(_rglru_body, chunk=chunk, merge=merge),
        name="rglru",
        grid=(B // merge, T // chunk),
        in_specs=[_seq_spec(merge, chunk, RG_P_W), _state_spec(merge, CARRY_ROWS, GROUP_W),
                  _state_spec(merge, 1, GROUP_W),
                  _param_spec((CONV_K, GROUP_W)), _param_spec((1, GROUP_W)),
                  _param_spec((GROUP_W, GROUP_W)), _param_spec((1, GROUP_W)),
                  _param_spec((GROUP_W, GROUP_W)), _param_spec((1, GROUP_W)), _param_spec((1, GROUP_W))],
        out_specs=[_seq_spec(merge, chunk, GROUP_W), _state_spec(merge, CARRY_ROWS, GROUP_W),
                   _state_spec(merge, 1, GROUP_W)],
        out_shape=[jax.ShapeDtypeStruct((B, T, GROUP_W), F32), jax.ShapeDtypeStruct((B, CARRY_ROWS, GROUP_W), F32),
                   jax.ShapeDtypeStruct((B, 1, GROUP_W), F32)],
        compiler_params=_mixer_params(),
    )(p, conv0, h0, lp["rg_conv_w"], lp["rg_conv_b"], lp["rg_wa"], lp["rg_ba"], lp["rg_wx"], lp["rg_bx"],
      lp["rg_lambda"])


def _block_diag(blocks):
    H, r, c = blocks.shape[-3:]
    eye = jnp.eye(H, dtype=blocks.dtype)
    out = blocks[..., :, :, None, :] * eye[:, None, :, None]
    return out.reshape(*blocks.shape[:-3], H * r, H * c)


def _diag_blocks(mat, H):
    r, c = mat.shape[-2] // H, mat.shape[-1] // H
    m = mat.reshape(*mat.shape[:-2], H, r, H, c)
    return jnp.stack([m[..., h, :, h, :] for h in range(H)], axis=-3)


def _pad_cols(w, offset, width):
    return jnp.pad(w, ((0, 0), (offset, width - offset - w.shape[1])))


def _pad_rows_to_lane(w, offset):
    return jnp.pad(w, ((offset, LANE - offset - w.shape[0]), (0, 0)))


def _layer_params(raw, i):
    g = lambda name: raw[name][i]
    row = lambda name: g(name).reshape(1, -1)
    w_in = g("w_in")
    o_rg, o_gdn = RWKV_COLS + 2 * GROUP_W, RWKV_COLS + 2 * GROUP_W + 4 * GROUP_W + 2 * N_HEADS
    gdn = w_in[:, o_rg:o_gdn]
    gla = w_in[:, o_gdn:]
    w_in_p = jnp.concatenate([
        w_in[:, :o_rg],
        gdn[:, :4 * GROUP_W], _pad_cols(gdn[:, 4 * GROUP_W:], 0, LANE),
        gla[:, :2 * GLA_QK_W + 2 * GROUP_W], _pad_cols(gla[:, 2 * GLA_QK_W + 2 * GROUP_W:], 0, LANE)], axis=1)
    lp = {
        "w_in": w_in_p.astype(BF16), "w_out": g("w_out").astype(BF16),
        "rwkv_w2": _pad_rows_to_lane(g("rwkv_w2"), 0),
        "rwkv_a2": _pad_rows_to_lane(g("rwkv_a2"), 32),
        "rwkv_g2": _pad_rows_to_lane(g("rwkv_g2"), 64),
        "rg_wa": _block_diag(g("rg_wa")), "rg_wx": _block_diag(g("rg_wx")),
        "gdn_conv_w": g("gdn_conv_w"), "rg_conv_w": g("rg_conv_w"),
        "gdn_A_log": jnp.repeat(g("gdn_A_log"), HEAD_DIM).reshape(1, -1),
        "gdn_dt_bias": jnp.repeat(g("gdn_dt_bias"), HEAD_DIM).reshape(1, -1),
        "gdn_norm_w": jnp.tile(g("gdn_norm_w"), N_HEADS).reshape(1, -1),
        "gla_norm_w": jnp.tile(g("gla_norm_w"), N_HEADS).reshape(1, -1),
        "gla_g2": _pad_rows_to_lane(g("gla_g2"), 0),
    }
    for name in ("ffn1_gate", "ffn1_up", "ffn1_down", "ffn2_gate", "ffn2_up", "ffn2_down"):
        lp[name] = g(name).astype(BF16)
    for name in ("norm_ffn1", "norm_mix", "norm_ffn2", "rwkv_mu", "rwkv_w0", "rwkv_a0", "rwkv_k_k", "rwkv_k_a",
                 "rwkv_r_k", "rwkv_ln_w", "rwkv_ln_b", "rg_conv_b", "rg_ba", "rg_bx", "rg_lambda", "gla_gb"):
        lp[name] = row(name)
    return lp


def _carry_tile(rows):
    return jnp.pad(rows, ((0, 0), (CARRY_ROWS - rows.shape[1], 0), (0, 0)))


def _run_trunk(x, states, lps, norm_final):
    B, T, _ = x.shape
    rows = B * T
    x2 = x.reshape(rows, D_MODEL)
    new_states = []
    for i, lp in enumerate(lps):
        if states is None:
            z = lambda *s: jnp.zeros((B,) + s, F32)
            sh0, rw0 = z(CARRY_ROWS, RWKV_COLS), z(GROUP_W, GROUP_W)
            rgc0, rgh0 = z(CARRY_ROWS, GROUP_W), z(1, GROUP_W)
            gdc0, gds0 = z(CARRY_ROWS, 3 * GROUP_W), z(GROUP_W, GROUP_W)
            gla0 = z(GROUP_W, GLA_QK_W)
        else:
            rw_shift, rw_S, rg_conv, rg_h, gdn_conv, gdn_S, gla_S = states[i]
            sh0 = _carry_tile(rw_shift[:, None, :])
            rw0 = _block_diag(rw_S)
            rgc0, rgh0 = _carry_tile(rg_conv), rg_h[:, None, :]
            gdc0, gds0 = _carry_tile(gdn_conv), _block_diag(gdn_S)
            gla0 = _block_diag(jnp.swapaxes(gla_S, -1, -2))
        x2, p_rw, p_rg, p_gdn, p_gla = _ffn_in(x2, lp)
        seq = lambda t: t.reshape(B, T, t.shape[-1])
        o_rw, sh1, rw1 = _rwkv(seq(p_rw), sh0, rw0, lp)
        o_rg, rgc1, rgh1 = _rglru(seq(p_rg), rgc0, rgh0, lp)
        o_gdn, gdc1, gds1 = _gdn(seq(p_gdn), gdc0, gds0, lp)
        o_gla, gla1 = _gla(seq(p_gla), gla0, lp)
        outs = [t.reshape(rows, GROUP_W) for t in (o_rw, o_rg, o_gdn, o_gla)]
        x2 = _out_ffn(x2, outs, lp, norm_final, final_norm=(i == len(lps) - 1))
        tail = CARRY_ROWS - (CONV_K - 1)
        new_states.append((sh1[:, CARRY_ROWS - 1], _diag_blocks(rw1, N_HEADS), rgc1[:, tail:], rgh1[:, 0],
                           gdc1[:, tail:], _diag_blocks(gds1, N_HEADS),
                           jnp.swapaxes(_diag_blocks(gla1, N_HEADS), -1, -2)))
    stacked = tuple(jnp.stack([st[j] for st in new_states]) for j in range(len(new_states[0])))
    return x2.reshape(B, T, D_MODEL), stacked


_STACKED = ("norm_ffn1", "ffn1_gate", "ffn1_up", "ffn1_down", "norm_mix", "w_in", "w_out", "rwkv_mu", "rwkv_w0",
            "rwkv_w2", "rwkv_a0", "rwkv_a2", "rwkv_g2", "rwkv_k_k", "rwkv_k_a", "rwkv_r_k", "rwkv_ln_w",
            "rwkv_ln_b", "rg_conv_w", "rg_conv_b", "rg_wa", "rg_ba", "rg_wx", "rg_bx", "rg_lambda", "gdn_conv_w",
            "gdn_A_log", "gdn_dt_bias", "gdn_norm_w", "gla_g2", "gla_gb", "gla_norm_w", "norm_ffn2", "ffn2_gate",
            "ffn2_up", "ffn2_down")


def kernel(x_prompt, x_sample, state_rwkv_shift, state_rwkv_wkv, state_rglru_conv, state_rglru_h, state_gdn_conv, state_gdn_S, state_gla_S, norm_ffn1, ffn1_gate, ffn1_up, ffn1_down, norm_mix, w_in, w_out, rwkv_mu, rwkv_w0, rwkv_w2, rwkv_a0, rwkv_a2, rwkv_g2, rwkv_k_k, rwkv_k_a, rwkv_r_k, rwkv_ln_w, rwkv_ln_b, rg_conv_w, rg_conv_b, rg_wa, rg_ba, rg_wx, rg_bx, rg_lambda, gdn_conv_w, gdn_A_log, gdn_dt_bias, gdn_norm_w, gla_g2, gla_gb, gla_norm_w, norm_ffn2, ffn2_gate, ffn2_up, ffn2_down, norm_final):
    values = (norm_ffn1, ffn1_gate, ffn1_up, ffn1_down, norm_mix, w_in, w_out, rwkv_mu, rwkv_w0, rwkv_w2, rwkv_a0,
              rwkv_a2, rwkv_g2, rwkv_k_k, rwkv_k_a, rwkv_r_k, rwkv_ln_w, rwkv_ln_b, rg_conv_w, rg_conv_b, rg_wa,
              rg_ba, rg_wx, rg_bx, rg_lambda, gdn_conv_w, gdn_A_log, gdn_dt_bias, gdn_norm_w, gla_g2, gla_gb,
              gla_norm_w, norm_ffn2, ffn2_gate, ffn2_up, ffn2_down)
    raw = dict(zip(_STACKED, values))
    depth = w_in.shape[0]
    lps = [_layer_params(raw, i) for i in range(depth)]
    nf = norm_final.reshape(1, -1)

    y_prompt, new_p = _run_trunk(x_prompt, None, lps, nf)
    sample_in = (state_rwkv_shift, state_rwkv_wkv, state_rglru_conv, state_rglru_h,
                 state_gdn_conv, state_gdn_S, state_gla_S)
    y_sample, new_s = _run_trunk(x_sample, [tuple(s[i] for s in sample_in) for i in range(depth)], lps, nf)
    return (y_prompt, y_sample) + new_p + new_s
```

```python
import functools

import jax
import jax.numpy as jnp
from jax import lax
from jax.experimental import pallas as pl
from jax.experimental.pallas import tpu as pltpu

F32 = jnp.float32
BF16 = jnp.bfloat16

D_MODEL = 1024
GROUP_W = 256
HEAD_DIM = 64
N_HEADS = 4
GLA_DK = 32
GLA_QK_W = N_HEADS * GLA_DK
RWKV_COLS = 896
CONV_K = 4
D_FF = 2816
NORM_EPS = 1e-6
RWKV_GN_EPS = 64e-5
RG_C = 8.0
GLA_TAU = 16.0
LANE = 128
SUBLANE = 8
CARRY_ROWS = SUBLANE
VMEM_LIMIT = 56 * 1024 * 1024
MIXER_VMEM_LIMIT = 32 * 1024 * 1024

GDN_P_W = 3 * GROUP_W + GROUP_W + LANE
GLA_P_W = 2 * GLA_QK_W + 2 * GROUP_W + LANE
RG_P_W = 2 * GROUP_W

FFN_TM = 512
FFN_TF = 1408
PROMPT_CHUNK = 64
SYSTEM_ROWS = 64
MAX_STATE_SEQS = 16
GLA_CHUNK = 16
GLA_ROWS = 128
RG_CHUNK = 256
MIX_PASSES = 1


def _dg(a, b, ca, cb):
    return lax.dot_general(a, b, (((ca,), (cb,)), ((), ())), preferred_element_type=F32)


def _split2(x):
    hi = x.astype(BF16)
    lo = (x - hi.astype(F32)).astype(BF16)
    return hi, lo


def _split3(x):
    hi = x.astype(BF16)
    r1 = x - hi.astype(F32)
    mid = r1.astype(BF16)
    lo = (r1 - mid.astype(F32)).astype(BF16)
    return hi, mid, lo


def _mm(a, b, ca=1, cb=0, passes=1):
    if passes == 1:
        return _dg(a.astype(BF16), b.astype(BF16), ca, cb)
    a1, a2 = _split2(a)
    b1, b2 = _split2(b)
    return _dg(a1, b1, ca, cb) + (_dg(a1, b2, ca, cb) + _dg(a2, b1, ca, cb))


def _mm_xc(x, c, ca=1, cb=0):
    x1, x2, x3 = _split3(x)
    return _dg(x1, c, ca, cb) + (_dg(x2, c, ca, cb) + _dg(x3, c, ca, cb))


def _mm_cx(c, x, ca=1, cb=0):
    x1, x2, x3 = _split3(x)
    return _dg(c, x1, ca, cb) + (_dg(c, x2, ca, cb) + _dg(c, x3, ca, cb))


def _iota(shape, dim):
    return lax.broadcasted_iota(jnp.int32, shape, dim)


def _log2(n):
    assert n & (n - 1) == 0
    return n.bit_length() - 1


def _head_masks(width, head_w):
    lane = _iota((1, width), 1)
    return [((lane >> _log2(head_w)) == h).astype(F32) for h in range(width // head_w)]


def _stack(x, masks):
    return jnp.concatenate([x * m for m in masks], axis=0)


def _unstack(xs, masks, rows):
    out = xs[0:rows] * masks[0]
    for h in range(1, len(masks)):
        out = out + xs[h * rows:(h + 1) * rows] * masks[h]
    return out


def _block_ones(rows, cols, rblk, cblk):
    r = _iota((rows, cols), 0) >> _log2(rblk)
    c = _iota((rows, cols), 1) >> _log2(cblk)
    return (r == c).astype(F32).astype(BF16)


def _tri_incl(rows, chunk):
    row = _iota((rows, rows), 0)
    col = _iota((rows, rows), 1)
    same = (row >> _log2(chunk)) == (col >> _log2(chunk))
    return (same & (col <= row)).astype(F32).astype(BF16)


def _softplus(x):
    return jnp.maximum(x, 0.0) + jnp.log1p(jnp.exp(-jnp.abs(x)))


def _sigmoid(x):
    return 1.0 / (1.0 + jnp.exp(-x))


def _silu(x):
    return x * _sigmoid(x)


def _rmsnorm(x, w):
    return x * lax.rsqrt(jnp.mean(x * x, axis=-1, keepdims=True) + NORM_EPS) * w


def _unit_lower_inverse(a, chunk, passes):
    n = a.shape[0]
    row = _iota((n, n), 0)
    col = _iota((n, n), 1)
    eye = (row == col).astype(F32)
    base = _log2(SUBLANE)
    a0 = jnp.where((row >> base) == (col >> base), a, 0.0)
    a2 = _mm(a0, a0, passes=passes)
    yield
    a4 = _mm(a2, a2, passes=passes)
    t = _mm(eye - a0, eye + a2, passes=passes)
    yield
    t = _mm(t, eye + a4, passes=passes)
    yield
    size = SUBLANE
    while size < chunk:
        lo = _log2(size)
        off = jnp.where(((row >> (lo + 1)) == (col >> (lo + 1))) & ((row >> lo) != (col >> lo)), a, 0.0)
        t_off = _mm(t, off, passes=passes)
        yield
        t = t - _mm(t_off, t, passes=passes)
        yield
        size *= 2
    return t


def _lockstep(generators):
    results = [None] * len(generators)
    live = list(enumerate(generators))
    while live:
        still = []
        for i, gen in live:
            try:
                next(gen)
                still.append((i, gen))
            except StopIteration as stop:
                results[i] = stop.value
        live = still
    return results


def _swiglu_half_step(x, nw, wg_ref, wu_ref, wd_ref):
    hb = _rmsnorm(x, nw).astype(BF16)
    acc = jnp.zeros_like(x)
    for c in range(D_FF // FFN_TF):
        sl = slice(c * FFN_TF, (c + 1) * FFN_TF)
        g = _dg(hb, wg_ref[:, sl], 1, 0)
        u = _dg(hb, wu_ref[:, sl], 1, 0)
        acc = acc + _dg((_silu(g) * u).astype(BF16), wd_ref[sl, :], 1, 0)
    return x + 0.5 * acc


def _ffn_in_body(x_ref, n1_ref, wg_ref, wu_ref, wd_ref, nm_ref, win_ref,
                 x_out, prw_out, prg_out, pgdn_out, pgla_out):
    x1 = _swiglu_half_step(x_ref[...], n1_ref[...], wg_ref, wu_ref, wd_ref)
    x_out[...] = x1
    hb = _rmsnorm(x1, nm_ref[...]).astype(BF16)
    off = 0
    for out in (prw_out, prg_out, pgdn_out, pgla_out):
        w = out.shape[-1]
        out[...] = _dg(hb, win_ref[:, off:off + w], 1, 0)
        off += w


def _out_ffn_body(x_ref, orw_ref, org_ref, ogdn_ref, ogla_ref, wo_ref, n2_ref, wg_ref, wu_ref, wd_ref,
                  nf_ref, y_out, *, final_norm):
    mixed = jnp.concatenate([orw_ref[...], org_ref[...], ogdn_ref[...], ogla_ref[...]], axis=-1)
    x2 = x_ref[...] + _dg(mixed.astype(BF16), wo_ref[...], 1, 0)
    y = _swiglu_half_step(x2, n2_ref[...], wg_ref, wu_ref, wd_ref)
    if final_norm:
        y = _rmsnorm(y, nf_ref[...])
    y_out[...] = y


def _resident(shape):
    return pl.BlockSpec(shape, lambda *_: (0,) * len(shape), pipeline_mode=pl.Buffered(1))


def _row_spec(width):
    return pl.BlockSpec((FFN_TM, width), lambda i: (i, 0))


def _dense_params():
    return pltpu.CompilerParams(dimension_semantics=("parallel",), vmem_limit_bytes=VMEM_LIMIT)


def _ffn_in(x, lp):
    rows = x.shape[0]
    widths = (RWKV_COLS, RG_P_W, GDN_P_W, GLA_P_W)
    return pl.pallas_call(
        _ffn_in_body,
        name="ffn_in",
        grid=(rows // FFN_TM,),
        in_specs=[_row_spec(D_MODEL), _resident((1, D_MODEL)),
                  _resident((D_MODEL, D_FF)), _resident((D_MODEL, D_FF)), _resident((D_FF, D_MODEL)),
                  _resident((1, D_MODEL)), _resident((D_MODEL, sum(widths)))],
        out_specs=[_row_spec(D_MODEL)] + [_row_spec(w) for w in widths],
        out_shape=[jax.ShapeDtypeStruct((rows, D_MODEL), F32)]
        + [jax.ShapeDtypeStruct((rows, w), F32) for w in widths],
        compiler_params=_dense_params(),
    )(x, lp["norm_ffn1"], lp["ffn1_gate"], lp["ffn1_up"], lp["ffn1_down"], lp["norm_mix"], lp["w_in"])


def _out_ffn(x, outs, lp, norm_final, final_norm):
    rows = x.shape[0]
    return pl.pallas_call(
        functools.partial(_out_ffn_body, final_norm=final_norm),
        name="out_ffn",
        grid=(rows // FFN_TM,),
        in_specs=[_row_spec(D_MODEL)] + [_row_spec(GROUP_W)] * 4
        + [_resident((D_MODEL, D_MODEL)), _resident((1, D_MODEL)),
           _resident((D_MODEL, D_FF)), _resident((D_MODEL, D_FF)), _resident((D_FF, D_MODEL)),
           _resident((1, D_MODEL))],
        out_specs=_row_spec(D_MODEL),
        out_shape=jax.ShapeDtypeStruct((rows, D_MODEL), F32),
        compiler_params=_dense_params(),
    )(x, *outs, lp["w_out"], lp["norm_ffn2"], lp["ffn2_gate"], lp["ffn2_up"], lp["ffn2_down"], norm_final)


def _seq_rows(ref, g0, merge):
    x3 = ref[g0:g0 + merge]
    return x3.reshape(x3.shape[0] * x3.shape[1], x3.shape[2])


def _carry_rows(ref, g0, merge, chunk):
    ext3 = ref[g0:g0 + merge]
    if chunk > CARRY_ROWS:
        ext3 = jnp.concatenate([ext3, jnp.zeros((merge, chunk - CARRY_ROWS, ext3.shape[2]), F32)], axis=1)
    return ext3.reshape(merge * chunk, ext3.shape[2])


def _shift_seq(x, ext, d, t_in):
    rows = x.shape[0]
    return jnp.where(t_in < d, pltpu.roll(ext, (rows - CARRY_ROWS + d) % rows, 0), pltpu.roll(x, d, 0))


def _causal_conv(x, ext, cw, t_in):
    y = cw[CONV_K - 1:CONV_K] * x
    for d in range(1, CONV_K):
        y = y + cw[CONV_K - 1 - d:CONV_K - d] * _shift_seq(x, ext, d, t_in)
    return y


def _store_seq(ref, g0, merge, chunk, x):
    ref[g0:g0 + merge] = x.reshape(merge, chunk, x.shape[1])


def _store_carry(ref, g0, merge, chunk, x):
    ref[g0:g0 + merge] = x.reshape(merge, chunk, x.shape[1])[:, chunk - CARRY_ROWS:chunk, :]


def _block_masks(n, chunk):
    row = _iota((n, n), 0)
    col = _iota((n, n), 1)
    same = (row >> _log2(chunk)) == (col >> _log2(chunk))
    return same & (col < row), same & (col <= row)


def _rwkv_body(p_ref, sh0_ref, s0_ref, mu_ref, w0_ref, w2_ref, a0_ref, a2_ref, g2_ref, kk_ref, ka_ref,
               rk_ref, lnw_ref, lnb_ref, o_ref, sh_ref, s_ref, *, chunk, merge, systems):
    @pl.when(pl.program_id(1) == 0)
    def _():
        s_ref[...] = s0_ref[...]
        sh_ref[...] = sh0_ref[...]

    P = MIX_PASSES
    L = chunk
    R = merge * L
    n = N_HEADS * R
    bones = _block_ones(GROUP_W, GROUP_W, HEAD_DIM, HEAD_DIM)
    blockdiag = bones.astype(F32)
    masks = _head_masks(GROUP_W, HEAD_DIM)
    strict, incl = _block_masks(n, L)
    tri = _tri_incl(R, L)
    t_in = _iota((R, RWKV_COLS), 0) & (L - 1)

    def system(p, ext, states):
        prev = _shift_seq(p, ext, 1, t_in)
        xm = p + (prev - p) * mu_ref[...]
        r = xm[:, 0:GROUP_W]
        k = xm[:, GROUP_W:2 * GROUP_W]
        v = xm[:, 2 * GROUP_W:3 * GROUP_W]
        x_lr = xm[:, 3 * GROUP_W:RWKV_COLS]
        w_log = -_softplus(-(w0_ref[...] + _mm(jnp.tanh(x_lr), w2_ref[...]))) - 0.5
        lw = -jnp.exp(w_log)
        a_icl = _sigmoid(a0_ref[...] + _mm(x_lr, a2_ref[...]))
        gate = _mm(_sigmoid(x_lr), g2_ref[...])
        kk = k * kk_ref[...]
        kk = kk * lax.rsqrt(_mm(kk * kk, bones) + 1e-12)
        k = k * (1.0 + (a_icl - 1.0) * ka_ref[...])
        b_vec = kk * a_icl
        yield

        lg = _mm_cx(tri, lw)
        dec_out = jnp.exp(-lg)
        a_x = -kk * jnp.exp(lg - lw)
        r_x = r * jnp.exp(lg)
        a_s = _stack(a_x, masks)
        r_s = _stack(r_x, masks)
        v_s = _stack(v, masks)
        bk_s = jnp.concatenate([_stack(b_vec * dec_out, masks), _stack(k * dec_out, masks)], axis=0)
        yield
        ar_bk = _mm(jnp.concatenate([a_s, r_s], axis=0), bk_s, 1, 1, P)
        a_ab = jnp.where(strict, ar_bk[0:n, 0:n], 0.0)
        a_ak = jnp.where(strict, ar_bk[0:n, n:2 * n], 0.0)
        m_rb = jnp.where(incl, ar_bk[n:2 * n, 0:n], 0.0)
        m_rk = jnp.where(incl, ar_bk[n:2 * n, n:2 * n], 0.0)

        za, zr = [], []
        for g, s in enumerate(states):
            rows = slice(g * L, (g + 1) * L)
            za.append(_mm(a_x[rows], s, 1, 1, P))
            zr.append(_mm(r_x[rows], s, 1, 1, P))
        za = za[0] if merge == 1 else jnp.concatenate(za, axis=0)
        zr = zr[0] if merge == 1 else jnp.concatenate(zr, axis=0)
        rhs_u = _stack(za, masks) + _mm(a_ak, v_s, passes=P)
        yield
        t_inv = yield from _unit_lower_inverse(-a_ab, L, P)
        u = _mm(t_inv, rhs_u, passes=P)
        yield
        uv = jnp.concatenate([u, v_s], axis=0)
        y = zr + _unstack(_mm(jnp.concatenate([m_rb, m_rk], axis=1), uv, passes=P), masks, R)
        u_x = _unstack(u, masks, R)
        yield
        new_states = []
        for g, s in enumerate(states):
            rows = slice(g * L, (g + 1) * L)
            lg_g = lg[rows]
            lg_last = lg_g[L - 1:L]
            carry = jnp.exp(lg_last - lg_g)
            lhs = jnp.concatenate([u_x[rows], v[rows]], axis=0)
            rhs = jnp.concatenate([b_vec[rows] * carry, k[rows] * carry], axis=0)
            new_states.append(s * jnp.exp(lg_last) + blockdiag * _mm(lhs, rhs, 0, 0, P))

        inv_n = 1.0 / HEAD_DIM
        mean = _mm(y, bones) * inv_n
        yield
        yc = y - mean
        var = _mm(yc * yc, bones) * inv_n
        yield
        yn = yc * lax.rsqrt(var + RWKV_GN_EPS) * lnw_ref[...] + lnb_ref[...]
        bonus = _mm(r * k * rk_ref[...], bones) * v
        return (yn + bonus) * gate, new_states

    loaded = [(_seq_rows(p_ref, i * merge, merge), _carry_rows(sh_ref, i * merge, merge, L),
               [s_ref[i * merge + g] for g in range(merge)]) for i in range(systems)]
    results = _lockstep([system(*args) for args in loaded])
    for i, ((p, _, _), (out, new_states)) in enumerate(zip(loaded, results)):
        _store_carry(sh_ref, i * merge, merge, L, p)
        _store_seq(o_ref, i * merge, merge, L, out)
        for g, s in enumerate(new_states):
            s_ref[i * merge + g] = s


def _gdn_body(p_ref, cv0_ref, s0_ref, cw_ref, alog_ref, dtb_ref, nw_ref, o_ref, cv_ref, s_ref, *,
              chunk, merge, systems):
    @pl.when(pl.program_id(1) == 0)
    def _():
        s_ref[...] = s0_ref[...]
        cv_ref[...] = cv0_ref[...]

    P = MIX_PASSES
    L = chunk
    R = merge * L
    n = N_HEADS * R
    bones = _block_ones(GROUP_W, GROUP_W, HEAD_DIM, HEAD_DIM)
    blockdiag = bones.astype(F32)
    masks = _head_masks(GROUP_W, HEAD_DIM)
    strict, incl = _block_masks(n, L)
    tri = _tri_incl(R, L)
    t_in = _iota((R, 3 * GROUP_W), 0) & (L - 1)
    src = _iota((LANE, GROUP_W), 0)
    dst = _iota((LANE, GROUP_W), 1) >> _log2(HEAD_DIM)
    spread_b = (src == dst).astype(F32).astype(BF16)
    spread_a = (src == dst + N_HEADS).astype(F32).astype(BF16)
    inv_w = 1.0 / HEAD_DIM

    def system(p, ext, states):
        qkv_in = p[:, 0:3 * GROUP_W]
        z = p[:, 3 * GROUP_W:4 * GROUP_W]
        ba = p[:, 4 * GROUP_W:GDN_P_W]
        qkv = _silu(_causal_conv(qkv_in, ext, cw_ref[...], t_in))
        q = qkv[:, 0:GROUP_W]
        k = qkv[:, GROUP_W:2 * GROUP_W]
        v = qkv[:, 2 * GROUP_W:3 * GROUP_W]
        q = q * lax.rsqrt(_mm(q * q, bones) + 1e-12) * (HEAD_DIM ** -0.5)
        k = k * lax.rsqrt(_mm(k * k, bones) + 1e-12)
        beta = _sigmoid(_mm_xc(ba, spread_b))
        g = -jnp.exp(alog_ref[...]) * _softplus(_mm_xc(ba, spread_a) + dtb_ref[...])
        yield

        gc = _mm_cx(tri, g)
        yield
        eg = jnp.exp(gc)
        k_s = _stack(k, masks)
        g_col = jnp.sum(_stack(gc, masks), axis=-1, keepdims=True) * inv_w
        beta_col = jnp.sum(_stack(beta, masks), axis=-1, keepdims=True) * inv_w
        g_row = jnp.transpose(jnp.broadcast_to(g_col, (n, LANE)))[0:1, :]
        decay = jnp.exp(jnp.minimum(g_col - g_row, 0.0))
        kq_k = _mm(jnp.concatenate([k_s, _stack(q, masks)], axis=0), k_s, 1, 1, P)
        yield
        a_mat = jnp.where(strict, beta_col * kq_k[0:n] * decay, 0.0)
        attn = jnp.where(incl, kq_k[n:2 * n] * decay, 0.0)
        rhs = jnp.concatenate([_stack(beta * eg * k, masks), _stack(beta * v, masks)], axis=1)
        t_inv = yield from _unit_lower_inverse(a_mat, L, P)
        wu = _mm(t_inv, rhs, passes=P)
        yield
        w_x = _unstack(wu[:, 0:GROUP_W], masks, R)
        qe = q * eg
        ws, qs = [], []
        for gi, s in enumerate(states):
            rows = slice(gi * L, (gi + 1) * L)
            ws.append(_mm(w_x[rows], s, passes=P))
            qs.append(_mm(qe[rows], s, passes=P))
        ws = ws[0] if merge == 1 else jnp.concatenate(ws, axis=0)
        qs = qs[0] if merge == 1 else jnp.concatenate(qs, axis=0)
        yield
        v_new = wu[:, GROUP_W:2 * GROUP_W] - _stack(ws, masks)
        o = qs + _unstack(_mm(attn, v_new, passes=P), masks, R)
        vn_x = _unstack(v_new, masks, R)
        yield
        new_states = []
        for gi, s in enumerate(states):
            rows = slice(gi * L, (gi + 1) * L)
            gc_g = gc[rows]
            g_last = gc_g[L - 1:L]
            k_hat = k[rows] * jnp.exp(g_last - gc_g)
            new_states.append(s * jnp.exp(g_last) + blockdiag * _mm(k_hat, vn_x[rows], 0, 0, P))

        o = o * lax.rsqrt(_mm(o * o, bones) * inv_w + NORM_EPS) * nw_ref[...] * _silu(z)
        return o, new_states

    loaded = [(_seq_rows(p_ref, i * merge, merge), _carry_rows(cv_ref, i * merge, merge, L),
               [s_ref[i * merge + g] for g in range(merge)]) for i in range(systems)]
    results = _lockstep([system(*args) for args in loaded])
    for i, ((p, _, _), (out, new_states)) in enumerate(zip(loaded, results)):
        _store_carry(cv_ref, i * merge, merge, L, p[:, 0:3 * GROUP_W])
        _store_seq(o_ref, i * merge, merge, L, out)
        for g, s in enumerate(new_states):
            s_ref[i * merge + g] = s


def _gla_body(p_ref, s0_ref, g2_ref, gb_ref, nw_ref, o_ref, s_ref, *, chunk, merge):
    @pl.when(pl.program_id(1) == 0)
    def _():
        s_ref[...] = s0_ref[...]

    P = MIX_PASSES
    L = chunk
    R = merge * L
    p = _seq_rows(p_ref, 0, merge)
    q = p[:, 0:GLA_QK_W] * (GLA_DK ** -0.5)
    k = p[:, GLA_QK_W:2 * GLA_QK_W]
    v = p[:, 2 * GLA_QK_W:2 * GLA_QK_W + GROUP_W]
    gt = p[:, 2 * GLA_QK_W + GROUP_W:2 * GLA_QK_W + 2 * GROUP_W]
    gd = p[:, 2 * GLA_QK_W + 2 * GROUP_W:GLA_P_W]
    gk = -_softplus(-(_mm(gd, g2_ref[...]) + gb_ref[...])) * (1.0 / GLA_TAU)
    bc = _mm_cx(_tri_incl(R, L), gk)
    spread = _block_ones(GLA_QK_W, GROUP_W, GLA_DK, HEAD_DIM)
    blockdiag = _block_ones(GROUP_W, GLA_QK_W, HEAD_DIM, GLA_DK).astype(F32)
    t_in = _iota((R, GROUP_W), 0) & (L - 1)

    def in_seq_row(x, j):
        x3 = x.reshape(merge, L, x.shape[1])
        return jnp.broadcast_to(x3[:, j:j + 1, :], x3.shape).reshape(x.shape)

    qe = q * jnp.exp(bc)
    o = []
    for g in range(merge):
        o.append(_mm(qe[g * L:(g + 1) * L], s_ref[g], 1, 1, P))
    o = o[0] if merge == 1 else jnp.concatenate(o, axis=0)
    for j in range(L):
        w = q * in_seq_row(k, j) * jnp.exp(jnp.minimum(bc - in_seq_row(bc, j), 0.0))
        o = o + jnp.where(t_in >= j, _mm(w, spread, passes=P), 0.0) * in_seq_row(v, j)
    for g in range(merge):
        rows = slice(g * L, (g + 1) * L)
        bc_g = bc[rows]
        b_last = bc_g[L - 1:L]
        k_hat = k[rows] * jnp.exp(b_last - bc_g)
        s_ref[g] = s_ref[g] * jnp.exp(b_last) + blockdiag * _mm(v[rows], k_hat, 0, 0, P)

    bones = _block_ones(GROUP_W, GROUP_W, HEAD_DIM, HEAD_DIM)
    o = o * lax.rsqrt(_mm(o * o, bones) * (1.0 / HEAD_DIM) + NORM_EPS) * nw_ref[...] * _silu(gt)
    _store_seq(o_ref, 0, merge, L, o)


def _rglru_body(p_ref, cv0_ref, h0_ref, cw_ref, cb_ref, wa_ref, ba_ref, wx_ref, bx_ref, lam_ref,
                o_ref, cv_ref, h_ref, *, chunk, merge):
    @pl.when(pl.program_id(1) == 0)
    def _():
        h_ref[...] = h0_ref[...]
        cv_ref[...] = cv0_ref[...]

    L = chunk
    R = merge * L
    p = _seq_rows(p_ref, 0, merge)
    xb = p[:, 0:GROUP_W]
    gate = p[:, GROUP_W:RG_P_W]
    t_in = _iota((R, GROUP_W), 0) & (L - 1)
    xc = _causal_conv(xb, _carry_rows(cv_ref, 0, merge, L), cw_ref[...], t_in) + cb_ref[...]
    _store_carry(cv_ref, 0, merge, L, xb)
    r = _sigmoid(_mm(xc, wa_ref[...]) + ba_ref[...])
    i = _sigmoid(_mm(xc, wx_ref[...]) + bx_ref[...])
    log_a = -RG_C * r * _softplus(-lam_ref[...])
    a = jnp.exp(log_a)
    b = jnp.sqrt(1.0 - jnp.exp(2.0 * log_a)) * (i * xc)
    d = 1
    while d < L:
        keep = t_in >= d
        a_sh = jnp.where(keep, pltpu.roll(a, d, 0), 1.0)
        b_sh = jnp.where(keep, pltpu.roll(b, d, 0), 0.0)
        b = a * b_sh + b
        a = a * a_sh
        d *= 2
    h3 = a.reshape(merge, L, GROUP_W) * h_ref[...] + b.reshape(merge, L, GROUP_W)
    h_ref[...] = h3[:, L - 1:L, :]
    h = h3.reshape(R, GROUP_W)
    gelu = 0.5 * gate * (1.0 + jnp.tanh(0.7978845608028654 * (gate + 0.044715 * gate * gate * gate)))
    _store_seq(o_ref, 0, merge, L, h * gelu)


def _mixer_params():
    return pltpu.CompilerParams(dimension_semantics=("parallel", "arbitrary"), vmem_limit_bytes=MIXER_VMEM_LIMIT)


def _seq_spec(seqs, rows, width):
    return pl.BlockSpec((seqs, rows, width), lambda b, c: (b, c, 0))


def _state_spec(seqs, rows, width):
    return pl.BlockSpec((seqs, rows, width), lambda b, c: (b, 0, 0))


def _param_spec(shape):
    return pl.BlockSpec(shape, lambda b, c: (0,) * len(shape))


def _delta_cfg(B, T):
    chunk = PROMPT_CHUNK if T % PROMPT_CHUNK == 0 else T
    merge = SYSTEM_ROWS // chunk
    systems = max(s for s in (8, 4, 2, 1) if B % (s * merge) == 0 and s * merge <= MAX_STATE_SEQS)
    return chunk, merge, systems


def _rwkv(p, shift0, s0, lp):
    B, T, _ = p.shape
    chunk, merge, systems = _delta_cfg(B, T)
    seqs = merge * systems
    row = lambda name: _param_spec((1, lp[name].shape[-1]))
    return pl.pallas_call(
        functools.partial(_rwkv_body, chunk=chunk, merge=merge, systems=systems),
        name="rwkv7",
        grid=(B // seqs, T // chunk),
        in_specs=[_seq_spec(seqs, chunk, RWKV_COLS), _state_spec(seqs, CARRY_ROWS, RWKV_COLS),
                  _state_spec(seqs, GROUP_W, GROUP_W),
                  row("rwkv_mu"), row("rwkv_w0"), _param_spec((LANE, GROUP_W)), row("rwkv_a0"),
                  _param_spec((LANE, GROUP_W)), _param_spec((LANE, GROUP_W)), row("rwkv_k_k"), row("rwkv_k_a"),
                  row("rwkv_r_k"), row("rwkv_ln_w"), row("rwkv_ln_b")],
        out_specs=[_seq_spec(seqs, chunk, GROUP_W), _state_spec(seqs, CARRY_ROWS, RWKV_COLS),
                   _state_spec(seqs, GROUP_W, GROUP_W)],
        out_shape=[jax.ShapeDtypeStruct((B, T, GROUP_W), F32), jax.ShapeDtypeStruct((B, CARRY_ROWS, RWKV_COLS), F32),
                   jax.ShapeDtypeStruct((B, GROUP_W, GROUP_W), F32)],
        compiler_params=_mixer_params(),
    )(p, shift0, s0, lp["rwkv_mu"], lp["rwkv_w0"], lp["rwkv_w2"], lp["rwkv_a0"], lp["rwkv_a2"], lp["rwkv_g2"],
      lp["rwkv_k_k"], lp["rwkv_k_a"], lp["rwkv_r_k"], lp["rwkv_ln_w"], lp["rwkv_ln_b"])


def _gdn(p, conv0, s0, lp):
    B, T, _ = p.shape
    chunk, merge, systems = _delta_cfg(B, T)
    seqs = merge * systems
    return pl.pallas_call(
        functools.partial(_gdn_body, chunk=chunk, merge=merge, systems=systems),
        name="gdn",
        grid=(B // seqs, T // chunk),
        in_specs=[_seq_spec(seqs, chunk, GDN_P_W), _state_spec(seqs, CARRY_ROWS, 3 * GROUP_W),
                  _state_spec(seqs, GROUP_W, GROUP_W),
                  _param_spec((CONV_K, 3 * GROUP_W)), _param_spec((1, GROUP_W)), _param_spec((1, GROUP_W)),
                  _param_spec((1, GROUP_W))],
        out_specs=[_seq_spec(seqs, chunk, GROUP_W), _state_spec(seqs, CARRY_ROWS, 3 * GROUP_W),
                   _state_spec(seqs, GROUP_W, GROUP_W)],
        out_shape=[jax.ShapeDtypeStruct((B, T, GROUP_W), F32),
                   jax.ShapeDtypeStruct((B, CARRY_ROWS, 3 * GROUP_W), F32),
                   jax.ShapeDtypeStruct((B, GROUP_W, GROUP_W), F32)],
        compiler_params=_mixer_params(),
    )(p, conv0, s0, lp["gdn_conv_w"], lp["gdn_A_log"], lp["gdn_dt_bias"], lp["gdn_norm_w"])


def _gla(p, s0, lp):
    B, T, _ = p.shape
    chunk = GLA_CHUNK if T % GLA_CHUNK == 0 else T
    merge = max(m for m in (16, 8, 4, 2, 1) if B % m == 0 and m * chunk <= GLA_ROWS)
    return pl.pallas_call(
        functools.partial(_gla_body, chunk=chunk, merge=merge),
        name="gla",
        grid=(B // merge, T // chunk),
        in_specs=[_seq_spec(merge, chunk, GLA_P_W), _state_spec(merge, GROUP_W, GLA_QK_W),
                  _param_spec((LANE, GLA_QK_W)), _param_spec((1, GLA_QK_W)), _param_spec((1, GROUP_W))],
        out_specs=[_seq_spec(merge, chunk, GROUP_W), _state_spec(merge, GROUP_W, GLA_QK_W)],
        out_shape=[jax.ShapeDtypeStruct((B, T, GROUP_W), F32), jax.ShapeDtypeStruct((B, GROUP_W, GLA_QK_W), F32)],
        compiler_params=_mixer_params(),
    )(p, s0, lp["gla_g2"], lp["gla_gb"], lp["gla_norm_w"])


def _rglru(p, conv0, h0, lp):
    B, T, _ = p.shape
    if T % RG_CHUNK == 0:
        chunk, merge = RG_CHUNK, 1
    else:
        assert T == CARRY_ROWS
        chunk, merge = T, B
    return pl.pallas_call(
        functools.partial(_rglru_body, chunk=chunk, merge=merge),
        name="rglru",
        grid=(B // merge, T // chunk),
        in_specs=[_seq_spec(merge, chunk, RG_P_W), _state_spec(merge, CARRY_ROWS, GROUP_W),
                  _state_spec(merge, 1, GROUP_W),
                  _param_spec((CONV_K, GROUP_W)), _param_spec((1, GROUP_W)),
                  _param_spec((GROUP_W, GROUP_W)), _param_spec((1, GROUP_W)),
                  _param_spec((GROUP_W, GROUP_W)), _param_spec((1, GROUP_W)), _param_spec((1, GROUP_W))],
        out_specs=[_seq_spec(merge, chunk, GROUP_W), _state_spec(merge, CARRY_ROWS, GROUP_W),
                   _state_spec(merge, 1, GROUP_W)],
        out_shape=[jax.ShapeDtypeStruct((B, T, GROUP_W), F32), jax.ShapeDtypeStruct((B, CARRY_ROWS, GROUP_W), F32),
                   jax.ShapeDtypeStruct((B, 1, GROUP_W), F32)],
        compiler_params=_mixer_params(),
    )(p, conv0, h0, lp["rg_conv_w"], lp["rg_conv_b"], lp["rg_wa"], lp["rg_ba"], lp["rg_wx"], lp["rg_bx"],
      lp["rg_lambda"])


def _block_diag(blocks):
    H, r, c = blocks.shape[-3:]
    lead = [(0, 0)] * (blocks.ndim - 3)
    rows = [jnp.pad(blocks[..., h, :, :], lead + [(0, 0), (h * c, (H - 1 - h) * c)]) for h in range(H)]
    return jnp.concatenate(rows, axis=-2)


def _diag_blocks(mat, H):
    r, c = mat.shape[-2] // H, mat.shape[-1] // H
    m = mat.reshape(*mat.shape[:-2], H, r, H, c)
    return jnp.stack([m[..., h, :, h, :] for h in range(H)], axis=-3)


def _pad_cols(w, offset, width):
    return jnp.pad(w, ((0, 0), (offset, width - offset - w.shape[1])))


def _pad_rows_to_lane(w, offset):
    return jnp.pad(w, ((offset, LANE - offset - w.shape[0]), (0, 0)))


def _layer_params(raw, i):
    g = lambda name: raw[name][i]
    row = lambda name: g(name).reshape(1, -1)
    w_in = g("w_in")
    o_rg, o_gdn = RWKV_COLS + 2 * GROUP_W, RWKV_COLS + 2 * GROUP_W + 4 * GROUP_W + 2 * N_HEADS
    gdn = w_in[:, o_rg:o_gdn]
    gla = w_in[:, o_gdn:]
    w_in_p = jnp.concatenate([
        w_in[:, :o_rg],
        gdn[:, :4 * GROUP_W], _pad_cols(gdn[:, 4 * GROUP_W:], 0, LANE),
        gla[:, :2 * GLA_QK_W + 2 * GROUP_W], _pad_cols(gla[:, 2 * GLA_QK_W + 2 * GROUP_W:], 0, LANE)], axis=1)
    lp = {
        "w_in": w_in_p.astype(BF16), "w_out": g("w_out").astype(BF16),
        "rwkv_w2": _pad_rows_to_lane(g("rwkv_w2"), 0),
        "rwkv_a2": _pad_rows_to_lane(g("rwkv_a2"), 32),
        "rwkv_g2": _pad_rows_to_lane(g("rwkv_g2"), 64),
        "rg_wa": _block_diag(g("rg_wa")), "rg_wx": _block_diag(g("rg_wx")),
        "gdn_conv_w": g("gdn_conv_w"), "rg_conv_w": g("rg_conv_w"),
        "gdn_A_log": jnp.repeat(g("gdn_A_log"), HEAD_DIM).reshape(1, -1),
        "gdn_dt_bias": jnp.repeat(g("gdn_dt_bias"), HEAD_DIM).reshape(1, -1),
        "gdn_norm_w": jnp.tile(g("gdn_norm_w"), N_HEADS).reshape(1, -1),
        "gla_norm_w": jnp.tile(g("gla_norm_w"), N_HEADS).reshape(1, -1),
        "gla_g2": _pad_rows_to_lane(g("gla_g2"), 0),
    }
    for name in ("ffn1_gate", "ffn1_up", "ffn1_down", "ffn2_gate", "ffn2_up", "ffn2_down"):
        lp[name] = g(name).astype(BF16)
    for name in ("norm_ffn1", "norm_mix", "norm_ffn2", "rwkv_mu", "rwkv_w0", "rwkv_a0", "rwkv_k_k", "rwkv_k_a",
                 "rwkv_r_k", "rwkv_ln_w", "rwkv_ln_b", "rg_conv_b", "rg_ba", "rg_bx", "rg_lambda", "gla_gb"):
        lp[name] = row(name)
    return lp


def _carry_tile(rows):
    return jnp.pad(rows, ((0, 0), (CARRY_ROWS - rows.shape[1], 0), (0, 0)))


def _run_trunk(x, states, lps, norm_final):
    B, T, _ = x.shape
    rows = B * T
    x2 = x.reshape(rows, D_MODEL)
    new_states = []
    for i, lp in enumerate(lps):
        if states is None:
            z = lambda *s: jnp.zeros((B,) + s, F32)
            sh0, rw0 = z(CARRY_ROWS, RWKV_COLS), z(GROUP_W, GROUP_W)
            rgc0, rgh0 = z(CARRY_ROWS, GROUP_W), z(1, GROUP_W)
            gdc0, gds0 = z(CARRY_ROWS, 3 * GROUP_W), z(GROUP_W, GROUP_W)
            gla0 = z(GROUP_W, GLA_QK_W)
        else:
            rw_shift, rw_S, rg_conv, rg_h, gdn_conv, gdn_S, gla_S = states[i]
            sh0 = _carry_tile(rw_shift[:, None, :])
            rw0 = _block_diag(rw_S)
            rgc0, rgh0 = _carry_tile(rg_conv), rg_h[:, None, :]
            gdc0, gds0 = _carry_tile(gdn_conv), _block_diag(gdn_S)
            gla0 = _block_diag(jnp.swapaxes(gla_S, -1, -2))
        x2, p_rw, p_rg, p_gdn, p_gla = _ffn_in(x2, lp)
        seq = lambda t: t.reshape(B, T, t.shape[-1])
        o_rw, sh1, rw1 = _rwkv(seq(p_rw), sh0, rw0, lp)
        o_rg, rgc1, rgh1 = _rglru(seq(p_rg), rgc0, rgh0, lp)
        o_gdn, gdc1, gds1 = _gdn(seq(p_gdn), gdc0, gds0, lp)
        o_gla, gla1 = _gla(seq(p_gla), gla0, lp)
        outs = [t.reshape(rows, GROUP_W) for t in (o_rw, o_rg, o_gdn, o_gla)]
        x2 = _out_ffn(x2, outs, lp, norm_final, final_norm=(i == len(lps) - 1))
        tail = CARRY_ROWS - (CONV_K - 1)
        new_states.append((sh1[:, CARRY_ROWS - 1], _diag_blocks(rw1, N_HEADS), rgc1[:, tail:], rgh1[:, 0],
                           gdc1[:, tail:], _diag_blocks(gds1, N_HEADS),
                           jnp.swapaxes(_diag_blocks(gla1, N_HEADS), -1, -2)))
    stacked = tuple(jnp.stack([st[j] for st in new_states]) for j in range(len(new_states[0])))
    return x2.reshape(B, T, D_MODEL), stacked


_STACKED = ("norm_ffn1", "ffn1_gate", "ffn1_up", "ffn1_down", "norm_mix", "w_in", "w_out", "rwkv_mu", "rwkv_w0",
            "rwkv_w2", "rwkv_a0", "rwkv_a2", "rwkv_g2", "rwkv_k_k", "rwkv_k_a", "rwkv_r_k", "rwkv_ln_w",
            "rwkv_ln_b", "rg_conv_w", "rg_conv_b", "rg_wa", "rg_ba", "rg_wx", "rg_bx", "rg_lambda", "gdn_conv_w",
            "gdn_A_log", "gdn_dt_bias", "gdn_norm_w", "gla_g2", "gla_gb", "gla_norm_w", "norm_ffn2", "ffn2_gate",
            "ffn2_up", "ffn2_down")


def kernel(x_prompt, x_sample, state_rwkv_shift, state_rwkv_wkv, state_rglru_conv, state_rglru_h, state_gdn_conv, state_gdn_S, state_gla_S, norm_ffn1, ffn1_gate, ffn1_up, ffn1_down, norm_mix, w_in, w_out, rwkv_mu, rwkv_w0, rwkv_w2, rwkv_a0, rwkv_a2, rwkv_g2, rwkv_k_k, rwkv_k_a, rwkv_r_k, rwkv_ln_w, rwkv_ln_b, rg_conv_w, rg_conv_b, rg_wa, rg_ba, rg_wx, rg_bx, rg_lambda, gdn_conv_w, gdn_A_log, gdn_dt_bias, gdn_norm_w, gla_g2, gla_gb, gla_norm_w, norm_ffn2, ffn2_gate, ffn2_up, ffn2_down, norm_final):
    values = (norm_ffn1, ffn1_gate, ffn1_up, ffn1_down, norm_mix, w_in, w_out, rwkv_mu, rwkv_w0, rwkv_w2, rwkv_a0,
              rwkv_a2, rwkv_g2, rwkv_k_k, rwkv_k_a, rwkv_r_k, rwkv_ln_w, rwkv_ln_b, rg_conv_w, rg_conv_b, rg_wa,
              rg_ba, rg_wx, rg_bx, rg_lambda, gdn_conv_w, gdn_A_log, gdn_dt_bias, gdn_norm_w, gla_g2, gla_gb,
              gla_norm_w, norm_ffn2, ffn2_gate, ffn2_up, ffn2_down)
    raw = dict(zip(_STACKED, values))
    depth = w_in.shape[0]
    lps = [_layer_params(raw, i) for i in range(depth)]
    nf = norm_final.reshape(1, -1)

    y_prompt, new_p = _run_trunk(x_prompt, None, lps, nf)
    sample_in = (state_rwkv_shift, state_rwkv_wkv, state_rglru_conv, state_rglru_h,
                 state_gdn_conv, state_gdn_S, state_gla_S)
    y_sample, new_s = _run_trunk(x_sample, [tuple(s[i] for s in sample_in) for i in range(depth)], lps, nf)
    return (y_prompt, y_sample) + new_p + new_s
```

```python
import functools

import jax
import jax.numpy as jnp
from jax import lax
from jax.experimental import pallas as pl
from jax.experimental.pallas import tpu as pltpu

F32 = jnp.float32
BF16 = jnp.bfloat16

D_MODEL = 1024
GROUP_W = 256
HEAD_DIM = 64
N_HEADS = 4
GLA_DK = 32
GLA_QK_W = N_HEADS * GLA_DK
RWKV_COLS = 896
CONV_K = 4
D_FF = 2816
NORM_EPS = 1e-6
RWKV_GN_EPS = 64e-5
RG_C = 8.0
GLA_TAU = 16.0
LANE = 128
SUBLANE = 8
CARRY_ROWS = SUBLANE
VMEM_LIMIT = 56 * 1024 * 1024
MIXER_VMEM_LIMIT = 32 * 1024 * 1024

GDN_P_W = 3 * GROUP_W + GROUP_W + LANE
GLA_P_W = 2 * GLA_QK_W + 2 * GROUP_W + LANE
RG_P_W = 2 * GROUP_W

FFN_TM = 512
FFN_TF = 1408
FFN_PARTS = 2
PROMPT_CHUNK = 64
SYSTEM_ROWS = 64
MAX_STATE_SEQS = 16
GLA_CHUNK = 16
GLA_ROWS = 128
RG_CHUNK = 256
MIX_PASSES = 1


def _dg(a, b, ca, cb):
    return lax.dot_general(a, b, (((ca,), (cb,)), ((), ())), preferred_element_type=F32)


def _split2(x):
    hi = x.astype(BF16)
    lo = (x - hi.astype(F32)).astype(BF16)
    return hi, lo


def _split3(x):
    hi = x.astype(BF16)
    r1 = x - hi.astype(F32)
    mid = r1.astype(BF16)
    lo = (r1 - mid.astype(F32)).astype(BF16)
    return hi, mid, lo


def _mm(a, b, ca=1, cb=0, passes=1):
    if passes == 1:
        return _dg(a.astype(BF16), b.astype(BF16), ca, cb)
    a1, a2 = _split2(a)
    b1, b2 = _split2(b)
    return _dg(a1, b1, ca, cb) + (_dg(a1, b2, ca, cb) + _dg(a2, b1, ca, cb))


def _mm_xc(x, c, ca=1, cb=0):
    x1, x2, x3 = _split3(x)
    return _dg(x1, c, ca, cb) + (_dg(x2, c, ca, cb) + _dg(x3, c, ca, cb))


def _mm_cx(c, x, ca=1, cb=0):
    x1, x2, x3 = _split3(x)
    return _dg(c, x1, ca, cb) + (_dg(c, x2, ca, cb) + _dg(c, x3, ca, cb))


def _iota(shape, dim):
    return lax.broadcasted_iota(jnp.int32, shape, dim)


def _log2(n):
    assert n & (n - 1) == 0
    return n.bit_length() - 1


def _head_masks(width, head_w):
    lane = _iota((1, width), 1)
    return [((lane >> _log2(head_w)) == h).astype(F32) for h in range(width // head_w)]


def _stack(x, masks):
    return jnp.concatenate([x * m for m in masks], axis=0)


def _unstack(xs, masks, rows):
    out = xs[0:rows] * masks[0]
    for h in range(1, len(masks)):
        out = out + xs[h * rows:(h + 1) * rows] * masks[h]
    return out


def _block_ones(rows, cols, rblk, cblk):
    r = _iota((rows, cols), 0) >> _log2(rblk)
    c = _iota((rows, cols), 1) >> _log2(cblk)
    return (r == c).astype(F32).astype(BF16)


def _tri_incl(rows, chunk):
    row = _iota((rows, rows), 0)
    col = _iota((rows, rows), 1)
    same = (row >> _log2(chunk)) == (col >> _log2(chunk))
    return (same & (col <= row)).astype(F32).astype(BF16)


def _softplus(x):
    return jnp.maximum(x, 0.0) + jnp.log1p(jnp.exp(-jnp.abs(x)))


def _sigmoid(x):
    return 1.0 / (1.0 + jnp.exp(-x))


def _silu(x):
    return x * _sigmoid(x)


def _rmsnorm(x, w):
    return x * lax.rsqrt(jnp.mean(x * x, axis=-1, keepdims=True) + NORM_EPS) * w


def _unit_lower_inverse(a, chunk, passes):
    n = a.shape[0]
    row = _iota((n, n), 0)
    col = _iota((n, n), 1)
    eye = (row == col).astype(F32)
    base = _log2(SUBLANE)
    a0 = jnp.where((row >> base) == (col >> base), a, 0.0)
    a2 = _mm(a0, a0, passes=passes)
    yield
    a4 = _mm(a2, a2, passes=passes)
    t = _mm(eye - a0, eye + a2, passes=passes)
    yield
    t = _mm(t, eye + a4, passes=passes)
    yield
    size = SUBLANE
    while size < chunk:
        lo = _log2(size)
        off = jnp.where(((row >> (lo + 1)) == (col >> (lo + 1))) & ((row >> lo) != (col >> lo)), a, 0.0)
        t_off = _mm(t, off, passes=passes)
        yield
        t = t - _mm(t_off, t, passes=passes)
        yield
        size *= 2
    return t


def _lockstep(generators):
    results = [None] * len(generators)
    live = list(enumerate(generators))
    while live:
        still = []
        for i, gen in live:
            try:
                next(gen)
                still.append((i, gen))
            except StopIteration as stop:
                results[i] = stop.value
        live = still
    return results


def _swiglu_half_step(x, nw, wg_ref, wu_ref, wd_ref):
    hb = _rmsnorm(x, nw).astype(BF16)
    yield
    acc = jnp.zeros_like(x)
    for c in range(D_FF // FFN_TF):
        sl = slice(c * FFN_TF, (c + 1) * FFN_TF)
        g = _dg(hb, wg_ref[:, sl], 1, 0)
        u = _dg(hb, wu_ref[:, sl], 1, 0)
        yield
        acc = acc + _dg((_silu(g) * u).astype(BF16), wd_ref[sl, :], 1, 0)
        yield
    return x + 0.5 * acc


def _row_parts():
    sub = FFN_TM // FFN_PARTS
    return [slice(i * sub, (i + 1) * sub) for i in range(FFN_PARTS)]


def _ffn_in_body(x_ref, n1_ref, wg_ref, wu_ref, wd_ref, nm_ref, win_ref,
                 x_out, prw_out, prg_out, pgdn_out, pgla_out):
    def part(rows):
        x1 = yield from _swiglu_half_step(x_ref[rows, :], n1_ref[...], wg_ref, wu_ref, wd_ref)
        x_out[rows, :] = x1
        hb = _rmsnorm(x1, nm_ref[...]).astype(BF16)
        yield
        off = 0
        for out in (prw_out, prg_out, pgdn_out, pgla_out):
            w = out.shape[-1]
            out[rows, :] = _dg(hb, win_ref[:, off:off + w], 1, 0)
            off += w
            yield

    _lockstep([part(rows) for rows in _row_parts()])


def _out_ffn_body(x_ref, orw_ref, org_ref, ogdn_ref, ogla_ref, wo_ref, n2_ref, wg_ref, wu_ref, wd_ref,
                  nf_ref, y_out, *, final_norm):
    def part(rows):
        mixed = jnp.concatenate([orw_ref[rows, :], org_ref[rows, :], ogdn_ref[rows, :], ogla_ref[rows, :]], axis=-1)
        x2 = x_ref[rows, :] + _dg(mixed.astype(BF16), wo_ref[...], 1, 0)
        yield
        y = yield from _swiglu_half_step(x2, n2_ref[...], wg_ref, wu_ref, wd_ref)
        if final_norm:
            y = _rmsnorm(y, nf_ref[...])
        y_out[rows, :] = y

    _lockstep([part(rows) for rows in _row_parts()])


def _resident(shape):
    return pl.BlockSpec(shape, lambda *_: (0,) * len(shape), pipeline_mode=pl.Buffered(1))


def _row_spec(width):
    return pl.BlockSpec((FFN_TM, width), lambda i: (i, 0))


def _dense_params():
    return pltpu.CompilerParams(dimension_semantics=("parallel",), vmem_limit_bytes=VMEM_LIMIT)


def _ffn_in(x, lp):
    rows = x.shape[0]
    widths = (RWKV_COLS, RG_P_W, GDN_P_W, GLA_P_W)
    return pl.pallas_call(
        _ffn_in_body,
        name="ffn_in",
        grid=(rows // FFN_TM,),
        in_specs=[_row_spec(D_MODEL), _resident((1, D_MODEL)),
                  _resident((D_MODEL, D_FF)), _resident((D_MODEL, D_FF)), _resident((D_FF, D_MODEL)),
                  _resident((1, D_MODEL)), _resident((D_MODEL, sum(widths)))],
        out_specs=[_row_spec(D_MODEL)] + [_row_spec(w) for w in widths],
        out_shape=[jax.ShapeDtypeStruct((rows, D_MODEL), F32)]
        + [jax.ShapeDtypeStruct((rows, w), F32) for w in widths],
        compiler_params=_dense_params(),
    )(x, lp["norm_ffn1"], lp["ffn1_gate"], lp["ffn1_up"], lp["ffn1_down"], lp["norm_mix"], lp["w_in"])


def _out_ffn(x, outs, lp, norm_final, final_norm):
    rows = x.shape[0]
    return pl.pallas_call(
        functools.partial(_out_ffn_body, final_norm=final_norm),
        name="out_ffn",
        grid=(rows // FFN_TM,),
        in_specs=[_row_spec(D_MODEL)] + [_row_spec(GROUP_W)] * 4
        + [_resident((D_MODEL, D_MODEL)), _resident((1, D_MODEL)),
           _resident((D_MODEL, D_FF)), _resident((D_MODEL, D_FF)), _resident((D_FF, D_MODEL)),
           _resident((1, D_MODEL))],
        out_specs=_row_spec(D_MODEL),
        out_shape=jax.ShapeDtypeStruct((rows, D_MODEL), F32),
        compiler_params=_dense_params(),
    )(x, *outs, lp["w_out"], lp["norm_ffn2"], lp["ffn2_gate"], lp["ffn2_up"], lp["ffn2_down"], norm_final)


def _seq_rows(ref, g0, merge):
    x3 = ref[g0:g0 + merge]
    return x3.reshape(x3.shape[0] * x3.shape[1], x3.shape[2])


def _carry_rows(ref, g0, merge, chunk):
    ext3 = ref[g0:g0 + merge]
    if chunk > CARRY_ROWS:
        ext3 = jnp.concatenate([ext3, jnp.zeros((merge, chunk - CARRY_ROWS, ext3.shape[2]), F32)], axis=1)
    return ext3.reshape(merge * chunk, ext3.shape[2])


def _shift_seq(x, ext, d, t_in):
    rows = x.shape[0]
    return jnp.where(t_in < d, pltpu.roll(ext, (rows - CARRY_ROWS + d) % rows, 0), pltpu.roll(x, d, 0))


def _causal_conv(x, ext, cw, t_in):
    y = cw[CONV_K - 1:CONV_K] * x
    for d in range(1, CONV_K):
        y = y + cw[CONV_K - 1 - d:CONV_K - d] * _shift_seq(x, ext, d, t_in)
    return y


def _store_seq(ref, g0, merge, chunk, x):
    ref[g0:g0 + merge] = x.reshape(merge, chunk, x.shape[1])


def _store_carry(ref, g0, merge, chunk, x):
    ref[g0:g0 + merge] = x.reshape(merge, chunk, x.shape[1])[:, chunk - CARRY_ROWS:chunk, :]


def _expand_states(compact_ref, full_ref):
    keep = _block_ones(GROUP_W, GROUP_W, HEAD_DIM, HEAD_DIM).astype(F32)
    for i in range(compact_ref.shape[0]):
        full_ref[i] = jnp.concatenate([compact_ref[i]] * N_HEADS, axis=1) * keep


def _compact_states(full_ref, compact_ref):
    for i in range(full_ref.shape[0]):
        s = full_ref[i]
        compact_ref[i] = jnp.concatenate(
            [s[h * HEAD_DIM:(h + 1) * HEAD_DIM, h * HEAD_DIM:(h + 1) * HEAD_DIM] for h in range(N_HEADS)], axis=0)


def _block_masks(n, chunk):
    row = _iota((n, n), 0)
    col = _iota((n, n), 1)
    same = (row >> _log2(chunk)) == (col >> _log2(chunk))
    return same & (col < row), same & (col <= row)


def _rwkv_body(p_ref, sh0_ref, s0_ref, mu_ref, w0_ref, w2_ref, a0_ref, a2_ref, g2_ref, kk_ref, ka_ref,
               rk_ref, lnw_ref, lnb_ref, o_ref, sh_ref, sc_ref, s_ref, *, chunk, merge, systems):
    @pl.when(pl.program_id(1) == 0)
    def _():
        _expand_states(s0_ref, s_ref)
        sh_ref[...] = sh0_ref[...]

    P = MIX_PASSES
    L = chunk
    R = merge * L
    n = N_HEADS * R
    bones = _block_ones(GROUP_W, GROUP_W, HEAD_DIM, HEAD_DIM)
    blockdiag = bones.astype(F32)
    masks = _head_masks(GROUP_W, HEAD_DIM)
    strict, incl = _block_masks(n, L)
    tri = _tri_incl(R, L)
    t_in = _iota((R, RWKV_COLS), 0) & (L - 1)

    def system(p, ext, states):
        prev = _shift_seq(p, ext, 1, t_in)
        xm = p + (prev - p) * mu_ref[...]
        r = xm[:, 0:GROUP_W]
        k = xm[:, GROUP_W:2 * GROUP_W]
        v = xm[:, 2 * GROUP_W:3 * GROUP_W]
        x_lr = xm[:, 3 * GROUP_W:RWKV_COLS]
        w_log = -_softplus(-(w0_ref[...] + _mm(jnp.tanh(x_lr), w2_ref[...]))) - 0.5
        lw = -jnp.exp(w_log)
        a_icl = _sigmoid(a0_ref[...] + _mm(x_lr, a2_ref[...]))
        gate = _mm(_sigmoid(x_lr), g2_ref[...])
        kk = k * kk_ref[...]
        kk = kk * lax.rsqrt(_mm(kk * kk, bones) + 1e-12)
        k = k * (1.0 + (a_icl - 1.0) * ka_ref[...])
        b_vec = kk * a_icl
        yield

        lg = _mm_cx(tri, lw)
        dec_out = jnp.exp(-lg)
        a_x = -kk * jnp.exp(lg - lw)
        r_x = r * jnp.exp(lg)
        a_s = _stack(a_x, masks)
        r_s = _stack(r_x, masks)
        v_s = _stack(v, masks)
        bk_s = jnp.concatenate([_stack(b_vec * dec_out, masks), _stack(k * dec_out, masks)], axis=0)
        yield
        ar_bk = _mm(jnp.concatenate([a_s, r_s], axis=0), bk_s, 1, 1, P)
        a_ab = jnp.where(strict, ar_bk[0:n, 0:n], 0.0)
        a_ak = jnp.where(strict, ar_bk[0:n, n:2 * n], 0.0)
        m_rb = jnp.where(incl, ar_bk[n:2 * n, 0:n], 0.0)
        m_rk = jnp.where(incl, ar_bk[n:2 * n, n:2 * n], 0.0)

        za, zr = [], []
        for g, s in enumerate(states):
            rows = slice(g * L, (g + 1) * L)
            za.append(_mm(a_x[rows], s, 1, 1, P))
            zr.append(_mm(r_x[rows], s, 1, 1, P))
        za = za[0] if merge == 1 else jnp.concatenate(za, axis=0)
        zr = zr[0] if merge == 1 else jnp.concatenate(zr, axis=0)
        rhs_u = _stack(za, masks) + _mm(a_ak, v_s, passes=P)
        yield
        t_inv = yield from _unit_lower_inverse(-a_ab, L, P)
        u = _mm(t_inv, rhs_u, passes=P)
        yield
        uv = jnp.concatenate([u, v_s], axis=0)
        y = zr + _unstack(_mm(jnp.concatenate([m_rb, m_rk], axis=1), uv, passes=P), masks, R)
        u_x = _unstack(u, masks, R)
        yield
        new_states = []
        for g, s in enumerate(states):
            rows = slice(g * L, (g + 1) * L)
            lg_g = lg[rows]
            lg_last = lg_g[L - 1:L]
            carry = jnp.exp(lg_last - lg_g)
            lhs = jnp.concatenate([u_x[rows], v[rows]], axis=0)
            rhs = jnp.concatenate([b_vec[rows] * carry, k[rows] * carry], axis=0)
            new_states.append(s * jnp.exp(lg_last) + blockdiag * _mm(lhs, rhs, 0, 0, P))

        inv_n = 1.0 / HEAD_DIM
        mean = _mm(y, bones) * inv_n
        yield
        yc = y - mean
        var = _mm(yc * yc, bones) * inv_n
        yield
        yn = yc * lax.rsqrt(var + RWKV_GN_EPS) * lnw_ref[...] + lnb_ref[...]
        bonus = _mm(r * k * rk_ref[...], bones) * v
        return (yn + bonus) * gate, new_states

    loaded = [(_seq_rows(p_ref, i * merge, merge), _carry_rows(sh_ref, i * merge, merge, L),
               [s_ref[i * merge + g] for g in range(merge)]) for i in range(systems)]
    results = _lockstep([system(*args) for args in loaded])
    for i, ((p, _, _), (out, new_states)) in enumerate(zip(loaded, results)):
        _store_carry(sh_ref, i * merge, merge, L, p)
        _store_seq(o_ref, i * merge, merge, L, out)
        for g, s in enumerate(new_states):
            s_ref[i * merge + g] = s

    @pl.when(pl.program_id(1) == pl.num_programs(1) - 1)
    def _():
        _compact_states(s_ref, sc_ref)


def _gdn_body(p_ref, cv0_ref, s0_ref, cw_ref, alog_ref, dtb_ref, nw_ref, o_ref, cv_ref, sc_ref, s_ref, *,
              chunk, merge, systems):
    @pl.when(pl.program_id(1) == 0)
    def _():
        _expand_states(s0_ref, s_ref)
        cv_ref[...] = cv0_ref[...]

    P = MIX_PASSES
    L = chunk
    R = merge * L
    n = N_HEADS * R
    bones = _block_ones(GROUP_W, GROUP_W, HEAD_DIM, HEAD_DIM)
    blockdiag = bones.astype(F32)
    masks = _head_masks(GROUP_W, HEAD_DIM)
    strict, incl = _block_masks(n, L)
    tri = _tri_incl(R, L)
    t_in = _iota((R, 3 * GROUP_W), 0) & (L - 1)
    src = _iota((LANE, GROUP_W), 0)
    dst = _iota((LANE, GROUP_W), 1) >> _log2(HEAD_DIM)
    spread_b = (src == dst).astype(F32).astype(BF16)
    spread_a = (src == dst + N_HEADS).astype(F32).astype(BF16)
    inv_w = 1.0 / HEAD_DIM

    def system(p, ext, states):
        qkv_in = p[:, 0:3 * GROUP_W]
        z = p[:, 3 * GROUP_W:4 * GROUP_W]
        ba = p[:, 4 * GROUP_W:GDN_P_W]
        qkv = _silu(_causal_conv(qkv_in, ext, cw_ref[...], t_in))
        q = qkv[:, 0:GROUP_W]
        k = qkv[:, GROUP_W:2 * GROUP_W]
        v = qkv[:, 2 * GROUP_W:3 * GROUP_W]
        q = q * lax.rsqrt(_mm(q * q, bones) + 1e-12) * (HEAD_DIM ** -0.5)
        k = k * lax.rsqrt(_mm(k * k, bones) + 1e-12)
        beta = _sigmoid(_mm_xc(ba, spread_b))
        g = -jnp.exp(alog_ref[...]) * _softplus(_mm_xc(ba, spread_a) + dtb_ref[...])
        yield

        gc = _mm_cx(tri, g)
        yield
        eg = jnp.exp(gc)
        k_s = _stack(k, masks)
        g_col = jnp.sum(_stack(gc, masks), axis=-1, keepdims=True) * inv_w
        beta_col = jnp.sum(_stack(beta, masks), axis=-1, keepdims=True) * inv_w
        g_row = jnp.transpose(jnp.broadcast_to(g_col, (n, LANE)))[0:1, :]
        decay = jnp.exp(jnp.minimum(g_col - g_row, 0.0))
        kq_k = _mm(jnp.concatenate([k_s, _stack(q, masks)], axis=0), k_s, 1, 1, P)
        yield
        a_mat = jnp.where(strict, beta_col * kq_k[0:n] * decay, 0.0)
        attn = jnp.where(incl, kq_k[n:2 * n] * decay, 0.0)
        rhs = jnp.concatenate([_stack(beta * eg * k, masks), _stack(beta * v, masks)], axis=1)
        t_inv = yield from _unit_lower_inverse(a_mat, L, P)
        wu = _mm(t_inv, rhs, passes=P)
        yield
        w_x = _unstack(wu[:, 0:GROUP_W], masks, R)
        qe = q * eg
        ws, qs = [], []
        for gi, s in enumerate(states):
            rows = slice(gi * L, (gi + 1) * L)
            ws.append(_mm(w_x[rows], s, passes=P))
            qs.append(_mm(qe[rows], s, passes=P))
        ws = ws[0] if merge == 1 else jnp.concatenate(ws, axis=0)
        qs = qs[0] if merge == 1 else jnp.concatenate(qs, axis=0)
        yield
        v_new = wu[:, GROUP_W:2 * GROUP_W] - _stack(ws, masks)
        o = qs + _unstack(_mm(attn, v_new, passes=P), masks, R)
        vn_x = _unstack(v_new, masks, R)
        yield
        new_states = []
        for gi, s in enumerate(states):
            rows = slice(gi * L, (gi + 1) * L)
            gc_g = gc[rows]
            g_last = gc_g[L - 1:L]
            k_hat = k[rows] * jnp.exp(g_last - gc_g)
            new_states.append(s * jnp.exp(g_last) + blockdiag * _mm(k_hat, vn_x[rows], 0, 0, P))

        o = o * lax.rsqrt(_mm(o * o, bones) * inv_w + NORM_EPS) * nw_ref[...] * _silu(z)
        return o, new_states

    loaded = [(_seq_rows(p_ref, i * merge, merge), _carry_rows(cv_ref, i * merge, merge, L),
               [s_ref[i * merge + g] for g in range(merge)]) for i in range(systems)]
    results = _lockstep([system(*args) for args in loaded])
    for i, ((p, _, _), (out, new_states)) in enumerate(zip(loaded, results)):
        _store_carry(cv_ref, i * merge, merge, L, p[:, 0:3 * GROUP_W])
        _store_seq(o_ref, i * merge, merge, L, out)
        for g, s in enumerate(new_states):
            s_ref[i * merge + g] = s

    @pl.when(pl.program_id(1) == pl.num_programs(1) - 1)
    def _():
        _compact_states(s_ref, sc_ref)


def _gla_body(p_ref, s0_ref, g2_ref, gb_ref, nw_ref, o_ref, s_ref, *, chunk, merge):
    @pl.when(pl.program_id(1) == 0)
    def _():
        s_ref[...] = s0_ref[...]

    P = MIX_PASSES
    L = chunk
    R = merge * L
    p = _seq_rows(p_ref, 0, merge)
    q = p[:, 0:GLA_QK_W] * (GLA_DK ** -0.5)
    k = p[:, GLA_QK_W:2 * GLA_QK_W]
    v = p[:, 2 * GLA_QK_W:2 * GLA_QK_W + GROUP_W]
    gt = p[:, 2 * GLA_QK_W + GROUP_W:2 * GLA_QK_W + 2 * GROUP_W]
    gd = p[:, 2 * GLA_QK_W + 2 * GROUP_W:GLA_P_W]
    gk = -_softplus(-(_mm(gd, g2_ref[...]) + gb_ref[...])) * (1.0 / GLA_TAU)
    bc = _mm_cx(_tri_incl(R, L), gk)
    spread = _block_ones(GLA_QK_W, GROUP_W, GLA_DK, HEAD_DIM)
    blockdiag = _block_ones(GROUP_W, GLA_QK_W, HEAD_DIM, GLA_DK).astype(F32)
    t_in = _iota((R, GROUP_W), 0) & (L - 1)

    def in_seq_row(x, j):
        x3 = x.reshape(merge, L, x.shape[1])
        return jnp.broadcast_to(x3[:, j:j + 1, :], x3.shape).reshape(x.shape)

    qe = q * jnp.exp(bc)
    o = []
    for g in range(merge):
        o.append(_mm(qe[g * L:(g + 1) * L], s_ref[g], 1, 1, P))
    o = o[0] if merge == 1 else jnp.concatenate(o, axis=0)
    for j in range(L):
        w = q * in_seq_row(k, j) * jnp.exp(jnp.minimum(bc - in_seq_row(bc, j), 0.0))
        o = o + jnp.where(t_in >= j, _mm(w, spread, passes=P), 0.0) * in_seq_row(v, j)
    for g in range(merge):
        rows = slice(g * L, (g + 1) * L)
        bc_g = bc[rows]
        b_last = bc_g[L - 1:L]
        k_hat = k[rows] * jnp.exp(b_last - bc_g)
        s_ref[g] = s_ref[g] * jnp.exp(b_last) + blockdiag * _mm(v[rows], k_hat, 0, 0, P)

    bones = _block_ones(GROUP_W, GROUP_W, HEAD_DIM, HEAD_DIM)
    o = o * lax.rsqrt(_mm(o * o, bones) * (1.0 / HEAD_DIM) + NORM_EPS) * nw_ref[...] * _silu(gt)
    _store_seq(o_ref, 0, merge, L, o)


def _rglru_body(p_ref, cv0_ref, h0_ref, cw_ref, cb_ref, wa_ref, ba_ref, wx_ref, bx_ref, lam_ref,
                o_ref, cv_ref, h_ref, *, chunk, merge):
    @pl.when(pl.program_id(1) == 0)
    def _():
        h_ref[...] = h0_ref[...]
        cv_ref[...] = cv0_ref[...]

    L = chunk
    R = merge * L
    p = _seq_rows(p_ref, 0, merge)
    xb = p[:, 0:GROUP_W]
    gate = p[:, GROUP_W:RG_P_W]
    t_in = _iota((R, GROUP_W), 0) & (L - 1)
    xc = _causal_conv(xb, _carry_rows(cv_ref, 0, merge, L), cw_ref[...], t_in) + cb_ref[...]
    _store_carry(cv_ref, 0, merge, L, xb)
    r = _sigmoid(_mm(xc, wa_ref[...]) + ba_ref[...])
    i = _sigmoid(_mm(xc, wx_ref[...]) + bx_ref[...])
    log_a = -RG_C * r * _softplus(-lam_ref[...])
    a = jnp.exp(log_a)
    b = jnp.sqrt(1.0 - jnp.exp(2.0 * log_a)) * (i * xc)
    d = 1
    while d < L:
        keep = t_in >= d
        a_sh = jnp.where(keep, pltpu.roll(a, d, 0), 1.0)
        b_sh = jnp.where(keep, pltpu.roll(b, d, 0), 0.0)
        b = a * b_sh + b
        a = a * a_sh
        d *= 2
    h3 = a.reshape(merge, L, GROUP_W) * h_ref[...] + b.reshape(merge, L, GROUP_W)
    h_ref[...] = h3[:, L - 1:L, :]
    h = h3.reshape(R, GROUP_W)
    gelu = 0.5 * gate * (1.0 + jnp.tanh(0.7978845608028654 * (gate + 0.044715 * gate * gate * gate)))
    _store_seq(o_ref, 0, merge, L, h * gelu)


def _mixer_params():
    return pltpu.CompilerParams(dimension_semantics=("parallel", "arbitrary"), vmem_limit_bytes=MIXER_VMEM_LIMIT)


def _seq_spec(seqs, rows, width):
    return pl.BlockSpec((seqs, rows, width), lambda b, c: (b, c, 0))


def _state_spec(seqs, rows, width):
    return pl.BlockSpec((seqs, rows, width), lambda b, c: (b, 0, 0))


def _param_spec(shape):
    return pl.BlockSpec(shape, lambda b, c: (0,) * len(shape))


def _delta_cfg(B, T):
    chunk = PROMPT_CHUNK if T % PROMPT_CHUNK == 0 else T
    merge = SYSTEM_ROWS // chunk
    systems = max(s for s in (8, 4, 2, 1) if B % (s * merge) == 0 and s * merge <= MAX_STATE_SEQS)
    return chunk, merge, systems


def _rwkv(p, shift0, s0, lp):
    B, T, _ = p.shape
    chunk, merge, systems = _delta_cfg(B, T)
    seqs = merge * systems
    row = lambda name: _param_spec((1, lp[name].shape[-1]))
    return pl.pallas_call(
        functools.partial(_rwkv_body, chunk=chunk, merge=merge, systems=systems),
        name="rwkv7",
        grid=(B // seqs, T // chunk),
        in_specs=[_seq_spec(seqs, chunk, RWKV_COLS), _state_spec(seqs, CARRY_ROWS, RWKV_COLS),
                  _state_spec(seqs, GROUP_W, HEAD_DIM),
                  row("rwkv_mu"), row("rwkv_w0"), _param_spec((LANE, GROUP_W)), row("rwkv_a0"),
                  _param_spec((LANE, GROUP_W)), _param_spec((LANE, GROUP_W)), row("rwkv_k_k"), row("rwkv_k_a"),
                  row("rwkv_r_k"), row("rwkv_ln_w"), row("rwkv_ln_b")],
        out_specs=[_seq_spec(seqs, chunk, GROUP_W), _state_spec(seqs, CARRY_ROWS, RWKV_COLS),
                   _state_spec(seqs, GROUP_W, HEAD_DIM)],
        out_shape=[jax.ShapeDtypeStruct((B, T, GROUP_W), F32), jax.ShapeDtypeStruct((B, CARRY_ROWS, RWKV_COLS), F32),
                   jax.ShapeDtypeStruct((B, GROUP_W, HEAD_DIM), F32)],
        scratch_shapes=[pltpu.VMEM((seqs, GROUP_W, GROUP_W), F32)],
        compiler_params=_mixer_params(),
    )(p, shift0, s0, lp["rwkv_mu"], lp["rwkv_w0"], lp["rwkv_w2"], lp["rwkv_a0"], lp["rwkv_a2"], lp["rwkv_g2"],
      lp["rwkv_k_k"], lp["rwkv_k_a"], lp["rwkv_r_k"], lp["rwkv_ln_w"], lp["rwkv_ln_b"])


def _gdn(p, conv0, s0, lp):
    B, T, _ = p.shape
    chunk, merge, systems = _delta_cfg(B, T)
    seqs = merge * systems
    return pl.pallas_call(
        functools.partial(_gdn_body, chunk=chunk, merge=merge, systems=systems),
        name="gdn",
        grid=(B // seqs, T // chunk),
        in_specs=[_seq_spec(seqs, chunk, GDN_P_W), _state_spec(seqs, CARRY_ROWS, 3 * GROUP_W),
                  _state_spec(seqs, GROUP_W, HEAD_DIM),
                  _param_spec((CONV_K, 3 * GROUP_W)), _param_spec((1, GROUP_W)), _param_spec((1, GROUP_W)),
                  _param_spec((1, GROUP_W))],
        out_specs=[_seq_spec(seqs, chunk, GROUP_W), _state_spec(seqs, CARRY_ROWS, 3 * GROUP_W),
                   _state_spec(seqs, GROUP_W, HEAD_DIM)],
        out_shape=[jax.ShapeDtypeStruct((B, T, GROUP_W), F32),
                   jax.ShapeDtypeStruct((B, CARRY_ROWS, 3 * GROUP_W), F32),
                   jax.ShapeDtypeStruct((B, GROUP_W, HEAD_DIM), F32)],
        scratch_shapes=[pltpu.VMEM((seqs, GROUP_W, GROUP_W), F32)],
        compiler_params=_mixer_params(),
    )(p, conv0, s0, lp["gdn_conv_w"], lp["gdn_A_log"], lp["gdn_dt_bias"], lp["gdn_norm_w"])


def _gla(p, s0, lp):
    B, T, _ = p.shape
    chunk = GLA_CHUNK if T % GLA_CHUNK == 0 else T
    merge = max(m for m in (16, 8, 4, 2, 1) if B % m == 0 and m * chunk <= GLA_ROWS)
    return pl.pallas_call(
        functools.partial(_gla_body, chunk=chunk, merge=merge),
        name="gla",
        grid=(B // merge, T // chunk),
        in_specs=[_seq_spec(merge, chunk, GLA_P_W), _state_spec(merge, GROUP_W, GLA_QK_W),
                  _param_spec((LANE, GLA_QK_W)), _param_spec((1, GLA_QK_W)), _param_spec((1, GROUP_W))],
        out_specs=[_seq_spec(merge, chunk, GROUP_W), _state_spec(merge, GROUP_W, GLA_QK_W)],
        out_shape=[jax.ShapeDtypeStruct((B, T, GROUP_W), F32), jax.ShapeDtypeStruct((B, GROUP_W, GLA_QK_W), F32)],
        compiler_params=_mixer_params(),
    )(p, s0, lp["gla_g2"], lp["gla_gb"], lp["gla_norm_w"])


def _rglru(p, conv0, h0, lp):
    B, T, _ = p.shape
    if T % RG_CHUNK == 0:
        chunk, merge = RG_CHUNK, 1
    else:
        assert T == CARRY_ROWS
        chunk, merge = T, B
    return pl.pallas_call(
        functools.partial(_rglru_body, chunk=chunk, merge=merge),
        name="rglru",
        grid=(B // merge, T // chunk),
        in_specs=[_seq_spec(merge, chunk, RG_P_W), _state_spec(merge, CARRY_ROWS, GROUP_W),
                  _state_spec(merge, 1, GROUP_W),
                  _param_spec((CONV_K, GROUP_W)), _param_spec((1, GROUP_W)),
                  _param_spec((GROUP_W, GROUP_W)), _param_spec((1, GROUP_W)),
                  _param_spec((GROUP_W, GROUP_W)), _param_spec((1, GROUP_W)), _param_spec((1, GROUP_W))],
        out_specs=[_seq_spec(merge, chunk, GROUP_W), _state_spec(merge, CARRY_ROWS, GROUP_W),
                   _state_spec(merge, 1, GROUP_W)],
        out_shape=[jax.ShapeDtypeStruct((B, T, GROUP_W), F32), jax.ShapeDtypeStruct((B, CARRY_ROWS, GROUP_W), F32),
                   jax.ShapeDtypeStruct((B, 1, GROUP_W), F32)],
        compiler_params=_mixer_params(),
    )(p, conv0, h0, lp["rg_conv_w"], lp["rg_conv_b"], lp["rg_wa"], lp["rg_ba"], lp["rg_wx"], lp["rg_bx"],
      lp["rg_lambda"])


def _block_diag(blocks):
    H, r, c = blocks.shape[-3:]
    lead = [(0, 0)] * (blocks.ndim - 3)
    rows = [jnp.pad(blocks[..., h, :, :], lead + [(0, 0), (h * c, (H - 1 - h) * c)]) for h in range(H)]
    return jnp.concatenate(rows, axis=-2)


def _diag_blocks(mat, H):
    r, c = mat.shape[-2] // H, mat.shape[-1] // H
    m = mat.reshape(*mat.shape[:-2], H, r, H, c)
    return jnp.stack([m[..., h, :, h, :] for h in range(H)], axis=-3)


def _pad_cols(w, offset, width):
    return jnp.pad(w, ((0, 0), (offset, width - offset - w.shape[1])))


def _pad_rows_to_lane(w, offset):
    return jnp.pad(w, ((offset, LANE - offset - w.shape[0]), (0, 0)))


def _layer_params(raw, i):
    g = lambda name: raw[name][i]
    row = lambda name: g(name).reshape(1, -1)
    w_in = g("w_in")
    o_rg, o_gdn = RWKV_COLS + 2 * GROUP_W, RWKV_COLS + 2 * GROUP_W + 4 * GROUP_W + 2 * N_HEADS
    gdn = w_in[:, o_rg:o_gdn]
    gla = w_in[:, o_gdn:]
    w_in_p = jnp.concatenate([
        w_in[:, :o_rg],
        gdn[:, :4 * GROUP_W], _pad_cols(gdn[:, 4 * GROUP_W:], 0, LANE),
        gla[:, :2 * GLA_QK_W + 2 * GROUP_W], _pad_cols(gla[:, 2 * GLA_QK_W + 2 * GROUP_W:], 0, LANE)], axis=1)
    lp = {
        "w_in": w_in_p.astype(BF16), "w_out": g("w_out").astype(BF16),
        "rwkv_w2": _pad_rows_to_lane(g("rwkv_w2"), 0),
        "rwkv_a2": _pad_rows_to_lane(g("rwkv_a2"), 32),
        "rwkv_g2": _pad_rows_to_lane(g("rwkv_g2"), 64),
        "rg_wa": _block_diag(g("rg_wa")), "rg_wx": _block_diag(g("rg_wx")),
        "gdn_conv_w": g("gdn_conv_w"), "rg_conv_w": g("rg_conv_w"),
        "gdn_A_log": jnp.repeat(g("gdn_A_log"), HEAD_DIM).reshape(1, -1),
        "gdn_dt_bias": jnp.repeat(g("gdn_dt_bias"), HEAD_DIM).reshape(1, -1),
        "gdn_norm_w": jnp.tile(g("gdn_norm_w"), N_HEADS).reshape(1, -1),
        "gla_norm_w": jnp.tile(g("gla_norm_w"), N_HEADS).reshape(1, -1),
        "gla_g2": _pad_rows_to_lane(g("gla_g2"), 0),
    }
    for name in ("ffn1_gate", "ffn1_up", "ffn1_down", "ffn2_gate", "ffn2_up", "ffn2_down"):
        lp[name] = g(name).astype(BF16)
    for name in ("norm_ffn1", "norm_mix", "norm_ffn2", "rwkv_mu", "rwkv_w0", "rwkv_a0", "rwkv_k_k", "rwkv_k_a",
                 "rwkv_r_k", "rwkv_ln_w", "rwkv_ln_b", "rg_conv_b", "rg_ba", "rg_bx", "rg_lambda", "gla_gb"):
        lp[name] = row(name)
    return lp


def _carry_tile(rows):
    return jnp.pad(rows, ((0, 0), (CARRY_ROWS - rows.shape[1], 0), (0, 0)))


def _run_trunk(x, states, lps, norm_final):
    B, T, _ = x.shape
    rows = B * T
    x2 = x.reshape(rows, D_MODEL)
    new_states = []
    for i, lp in enumerate(lps):
        if states is None:
            z = lambda *s: jnp.zeros((B,) + s, F32)
            sh0, rw0 = z(CARRY_ROWS, RWKV_COLS), z(GROUP_W, HEAD_DIM)
            rgc0, rgh0 = z(CARRY_ROWS, GROUP_W), z(1, GROUP_W)
            gdc0, gds0 = z(CARRY_ROWS, 3 * GROUP_W), z(GROUP_W, HEAD_DIM)
            gla0 = z(GROUP_W, GLA_QK_W)
        else:
            rw_shift, rw_S, rg_conv, rg_h, gdn_conv, gdn_S, gla_S = states[i]
            sh0 = _carry_tile(rw_shift[:, None, :])
            rw0 = rw_S.reshape(B, GROUP_W, HEAD_DIM)
            rgc0, rgh0 = _carry_tile(rg_conv), rg_h[:, None, :]
            gdc0, gds0 = _carry_tile(gdn_conv), gdn_S.reshape(B, GROUP_W, HEAD_DIM)
            gla0 = _block_diag(jnp.swapaxes(gla_S, -1, -2))
        x2, p_rw, p_rg, p_gdn, p_gla = _ffn_in(x2, lp)
        seq = lambda t: t.reshape(B, T, t.shape[-1])
        o_rw, sh1, rw1 = _rwkv(seq(p_rw), sh0, rw0, lp)
        o_rg, rgc1, rgh1 = _rglru(seq(p_rg), rgc0, rgh0, lp)
        o_gdn, gdc1, gds1 = _gdn(seq(p_gdn), gdc0, gds0, lp)
        o_gla, gla1 = _gla(seq(p_gla), gla0, lp)
        outs = [t.reshape(rows, GROUP_W) for t in (o_rw, o_rg, o_gdn, o_gla)]
        x2 = _out_ffn(x2, outs, lp, norm_final, final_norm=(i == len(lps) - 1))
        tail = CARRY_ROWS - (CONV_K - 1)
        heads = (B, N_HEADS, HEAD_DIM, HEAD_DIM)
        new_states.append((sh1[:, CARRY_ROWS - 1], rw1.reshape(heads), rgc1[:, tail:], rgh1[:, 0],
                           gdc1[:, tail:], gds1.reshape(heads),
                           jnp.swapaxes(_diag_blocks(gla1, N_HEADS), -1, -2)))
    stacked = tuple(jnp.stack([st[j] for st in new_states]) for j in range(len(new_states[0])))
    return x2.reshape(B, T, D_MODEL), stacked


_STACKED = ("norm_ffn1", "ffn1_gate", "ffn1_up", "ffn1_down", "norm_mix", "w_in", "w_out", "rwkv_mu", "rwkv_w0",
            "rwkv_w2", "rwkv_a0", "rwkv_a2", "rwkv_g2", "rwkv_k_k", "rwkv_k_a", "rwkv_r_k", "rwkv_ln_w",
            "rwkv_ln_b", "rg_conv_w", "rg_conv_b", "rg_wa", "rg_ba", "rg_wx", "rg_bx", "rg_lambda", "gdn_conv_w",
            "gdn_A_log", "gdn_dt_bias", "gdn_norm_w", "gla_g2", "gla_gb", "gla_norm_w", "norm_ffn2", "ffn2_gate",
            "ffn2_up", "ffn2_down")


def kernel(x_prompt, x_sample, state_rwkv_shift, state_rwkv_wkv, state_rglru_conv, state_rglru_h, state_gdn_conv, state_gdn_S, state_gla_S, norm_ffn1, ffn1_gate, ffn1_up, ffn1_down, norm_mix, w_in, w_out, rwkv_mu, rwkv_w0, rwkv_w2, rwkv_a0, rwkv_a2, rwkv_g2, rwkv_k_k, rwkv_k_a, rwkv_r_k, rwkv_ln_w, rwkv_ln_b, rg_conv_w, rg_conv_b, rg_wa, rg_ba, rg_wx, rg_bx, rg_lambda, gdn_conv_w, gdn_A_log, gdn_dt_bias, gdn_norm_w, gla_g2, gla_gb, gla_norm_w, norm_ffn2, ffn2_gate, ffn2_up, ffn2_down, norm_final):
    values = (norm_ffn1, ffn1_gate, ffn1_up, ffn1_down, norm_mix, w_in, w_out, rwkv_mu, rwkv_w0, rwkv_w2, rwkv_a0,
              rwkv_a2, rwkv_g2, rwkv_k_k, rwkv_k_a, rwkv_r_k, rwkv_ln_w, rwkv_ln_b, rg_conv_w, rg_conv_b, rg_wa,
              rg_ba, rg_wx, rg_bx, rg_lambda, gdn_conv_w, gdn_A_log, gdn_dt_bias, gdn_norm_w, gla_g2, gla_gb,
              gla_norm_w, norm_ffn2, ffn2_gate, ffn2_up, ffn2_down)
    raw = dict(zip(_STACKED, values))
    depth = w_in.shape[0]
    lps = [_layer_params(raw, i) for i in range(depth)]
    nf = norm_final.reshape(1, -1)

    y_prompt, new_p = _run_trunk(x_prompt, None, lps, nf)
    sample_in = (state_rwkv_shift, state_rwkv_wkv, state_rglru_conv, state_rglru_h,
                 state_gdn_conv, state_gdn_S, state_gla_S)
    y_sample, new_s = _run_trunk(x_sample, [tuple(s[i] for s in sample_in) for i in range(depth)], lps, nf)
    return (y_prompt, y_sample) + new_p + new_s
```

```python
import functools

import jax
import jax.numpy as jnp
from jax import lax
from jax.experimental import pallas as pl
from jax.experimental.pallas import tpu as pltpu

F32 = jnp.float32
BF16 = jnp.bfloat16

D_MODEL = 1024
GROUP_W = 256
HEAD_DIM = 64
N_HEADS = 4
GLA_DK = 32
GLA_QK_W = N_HEADS * GLA_DK
RWKV_COLS = 896
CONV_K = 4
D_FF = 2816
NORM_EPS = 1e-6
RWKV_GN_EPS = 64e-5
RG_C = 8.0
GLA_TAU = 16.0
LANE = 128
SUBLANE = 8
CARRY_ROWS = SUBLANE
VMEM_LIMIT = 56 * 1024 * 1024
MIXER_VMEM_LIMIT = 32 * 1024 * 1024

GDN_P_W = 3 * GROUP_W + GROUP_W + LANE
GLA_P_W = 2 * GLA_QK_W + 2 * GROUP_W + LANE
RG_P_W = 2 * GROUP_W

FFN_TM = 512
FFN_TF = 1408
FFN_PARTS = 2
PROMPT_CHUNK = 64
SYSTEM_ROWS = 64
MAX_STATE_SEQS = 16
GLA_CHUNK = 16
GLA_ROWS = 128
RG_CHUNK = 256
MIX_PASSES = 1


def _dg(a, b, ca, cb):
    return lax.dot_general(a, b, (((ca,), (cb,)), ((), ())), preferred_element_type=F32)


def _split2(x):
    hi = x.astype(BF16)
    lo = (x - hi.astype(F32)).astype(BF16)
    return hi, lo


def _split3(x):
    hi = x.astype(BF16)
    r1 = x - hi.astype(F32)
    mid = r1.astype(BF16)
    lo = (r1 - mid.astype(F32)).astype(BF16)
    return hi, mid, lo


def _mm(a, b, ca=1, cb=0, passes=1):
    if passes == 1:
        return _dg(a.astype(BF16), b.astype(BF16), ca, cb)
    a1, a2 = _split2(a)
    b1, b2 = _split2(b)
    return _dg(a1, b1, ca, cb) + (_dg(a1, b2, ca, cb) + _dg(a2, b1, ca, cb))


def _mm_xc(x, c, ca=1, cb=0):
    x1, x2, x3 = _split3(x)
    return _dg(x1, c, ca, cb) + (_dg(x2, c, ca, cb) + _dg(x3, c, ca, cb))


def _mm_cx(c, x, ca=1, cb=0):
    x1, x2, x3 = _split3(x)
    return _dg(c, x1, ca, cb) + (_dg(c, x2, ca, cb) + _dg(c, x3, ca, cb))


def _iota(shape, dim):
    return lax.broadcasted_iota(jnp.int32, shape, dim)


def _log2(n):
    assert n & (n - 1) == 0
    return n.bit_length() - 1


def _head_masks(width, head_w):
    lane = _iota((1, width), 1)
    return [((lane >> _log2(head_w)) == h).astype(F32) for h in range(width // head_w)]


def _stack(x, masks):
    return jnp.concatenate([x * m for m in masks], axis=0)


def _unstack(xs, masks, rows):
    out = xs[0:rows] * masks[0]
    for h in range(1, len(masks)):
        out = out + xs[h * rows:(h + 1) * rows] * masks[h]
    return out


def _block_ones(rows, cols, rblk, cblk):
    r = _iota((rows, cols), 0) >> _log2(rblk)
    c = _iota((rows, cols), 1) >> _log2(cblk)
    return (r == c).astype(F32).astype(BF16)


def _tri_incl(rows, chunk):
    row = _iota((rows, rows), 0)
    col = _iota((rows, rows), 1)
    same = (row >> _log2(chunk)) == (col >> _log2(chunk))
    return (same & (col <= row)).astype(F32).astype(BF16)


def _softplus(x):
    return jnp.maximum(x, 0.0) + jnp.log1p(jnp.exp(-jnp.abs(x)))


def _sigmoid(x):
    return 1.0 / (1.0 + jnp.exp(-x))


def _silu(x):
    return x * _sigmoid(x)


def _rmsnorm(x, w):
    return x * lax.rsqrt(jnp.mean(x * x, axis=-1, keepdims=True) + NORM_EPS) * w


def _unit_lower_inverse(a, chunk, passes):
    n = a.shape[0]
    row = _iota((n, n), 0)
    col = _iota((n, n), 1)
    eye = (row == col).astype(F32)
    base = _log2(SUBLANE)
    a0 = jnp.where((row >> base) == (col >> base), a, 0.0)
    a2 = _mm(a0, a0, passes=passes)
    yield
    a4 = _mm(a2, a2, passes=passes)
    t = _mm(eye - a0, eye + a2, passes=passes)
    yield
    t = _mm(t, eye + a4, passes=passes)
    yield
    size = SUBLANE
    while size < chunk:
        lo = _log2(size)
        off = jnp.where(((row >> (lo + 1)) == (col >> (lo + 1))) & ((row >> lo) != (col >> lo)), a, 0.0)
        t_off = _mm(t, off, passes=passes)
        yield
        t = t - _mm(t_off, t, passes=passes)
        yield
        size *= 2
    return t


def _lockstep(generators):
    results = [None] * len(generators)
    live = list(enumerate(generators))
    while live:
        still = []
        for i, gen in live:
            try:
                next(gen)
                still.append((i, gen))
            except StopIteration as stop:
                results[i] = stop.value
        live = still
    return results


def _swiglu_half_step(x, nw, wg_ref, wu_ref, wd_ref):
    hb = _rmsnorm(x, nw).astype(BF16)
    yield
    acc = jnp.zeros_like(x)
    for c in range(D_FF // FFN_TF):
        sl = slice(c * FFN_TF, (c + 1) * FFN_TF)
        g = _dg(hb, wg_ref[:, sl], 1, 0)
        u = _dg(hb, wu_ref[:, sl], 1, 0)
        yield
        acc = acc + _dg((_silu(g) * u).astype(BF16), wd_ref[sl, :], 1, 0)
        yield
    return x + 0.5 * acc


def _row_parts():
    sub = FFN_TM // FFN_PARTS
    return [slice(i * sub, (i + 1) * sub) for i in range(FFN_PARTS)]


def _ffn_in_body(x_ref, n1_ref, wg_ref, wu_ref, wd_ref, nm_ref, win_ref,
                 x_out, prw_out, prg_out, pgdn_out, pgla_out):
    def part(rows):
        x1 = yield from _swiglu_half_step(x_ref[rows, :], n1_ref[...], wg_ref, wu_ref, wd_ref)
        x_out[rows, :] = x1
        hb = _rmsnorm(x1, nm_ref[...]).astype(BF16)
        yield
        off = 0
        for out in (prw_out, prg_out, pgdn_out, pgla_out):
            w = out.shape[-1]
            out[rows, :] = _dg(hb, win_ref[:, off:off + w], 1, 0)
            off += w
            yield

    _lockstep([part(rows) for rows in _row_parts()])


def _out_ffn_body(x_ref, orw_ref, org_ref, ogdn_ref, ogla_ref, wo_ref, n2_ref, wg_ref, wu_ref, wd_ref,
                  nf_ref, y_out, *, final_norm):
    def part(rows):
        mixed = jnp.concatenate([orw_ref[rows, :], org_ref[rows, :], ogdn_ref[rows, :], ogla_ref[rows, :]], axis=-1)
        x2 = x_ref[rows, :] + _dg(mixed.astype(BF16), wo_ref[...], 1, 0)
        yield
        y = yield from _swiglu_half_step(x2, n2_ref[...], wg_ref, wu_ref, wd_ref)
        if final_norm:
            y = _rmsnorm(y, nf_ref[...])
        y_out[rows, :] = y

    _lockstep([part(rows) for rows in _row_parts()])


def _resident(shape, layer=None):
    if layer is None:
        return pl.BlockSpec(shape, lambda *_: (0,) * len(shape), pipeline_mode=pl.Buffered(1))
    return pl.BlockSpec((None,) + shape, lambda *_: (layer,) + (0,) * len(shape), pipeline_mode=pl.Buffered(1))


def _row_spec(width):
    return pl.BlockSpec((FFN_TM, width), lambda i: (i, 0))


def _dense_params():
    return pltpu.CompilerParams(dimension_semantics=("parallel",), vmem_limit_bytes=VMEM_LIMIT)


def _ffn_in(x, lp):
    rows = x.shape[0]
    layer = lp["layer"]
    widths = (RWKV_COLS, RG_P_W, GDN_P_W, GLA_P_W)
    return pl.pallas_call(
        _ffn_in_body,
        name="ffn_in",
        grid=(rows // FFN_TM,),
        in_specs=[_row_spec(D_MODEL), _resident((1, D_MODEL)),
                  _resident((D_MODEL, D_FF), layer), _resident((D_MODEL, D_FF), layer),
                  _resident((D_FF, D_MODEL), layer),
                  _resident((1, D_MODEL)), _resident((D_MODEL, sum(widths)), layer)],
        out_specs=[_row_spec(D_MODEL)] + [_row_spec(w) for w in widths],
        out_shape=[jax.ShapeDtypeStruct((rows, D_MODEL), F32)]
        + [jax.ShapeDtypeStruct((rows, w), F32) for w in widths],
        compiler_params=_dense_params(),
    )(x, lp["norm_ffn1"], lp["ffn1_gate"], lp["ffn1_up"], lp["ffn1_down"], lp["norm_mix"], lp["w_in"])


def _out_ffn(x, outs, lp, norm_final, final_norm):
    rows = x.shape[0]
    layer = lp["layer"]
    return pl.pallas_call(
        functools.partial(_out_ffn_body, final_norm=final_norm),
        name="out_ffn",
        grid=(rows // FFN_TM,),
        in_specs=[_row_spec(D_MODEL)] + [_row_spec(GROUP_W)] * 4
        + [_resident((D_MODEL, D_MODEL), layer), _resident((1, D_MODEL)),
           _resident((D_MODEL, D_FF), layer), _resident((D_MODEL, D_FF), layer), _resident((D_FF, D_MODEL), layer),
           _resident((1, D_MODEL))],
        out_specs=_row_spec(D_MODEL),
        out_shape=jax.ShapeDtypeStruct((rows, D_MODEL), F32),
        compiler_params=_dense_params(),
    )(x, *outs, lp["w_out"], lp["norm_ffn2"], lp["ffn2_gate"], lp["ffn2_up"], lp["ffn2_down"], norm_final)


def _seq_rows(ref, g0, merge):
    x3 = ref[g0:g0 + merge]
    return x3.reshape(x3.shape[0] * x3.shape[1], x3.shape[2])


def _carry_rows(ref, g0, merge, chunk):
    ext3 = ref[g0:g0 + merge]
    if chunk > CARRY_ROWS:
        ext3 = jnp.concatenate([ext3, jnp.zeros((merge, chunk - CARRY_ROWS, ext3.shape[2]), F32)], axis=1)
    return ext3.reshape(merge * chunk, ext3.shape[2])


def _shift_seq(x, ext, d, t_in):
    rows = x.shape[0]
    return jnp.where(t_in < d, pltpu.roll(ext, (rows - CARRY_ROWS + d) % rows, 0), pltpu.roll(x, d, 0))


def _causal_conv(x, ext, cw, t_in):
    y = cw[CONV_K - 1:CONV_K] * x
    for d in range(1, CONV_K):
        y = y + cw[CONV_K - 1 - d:CONV_K - d] * _shift_seq(x, ext, d, t_in)
    return y


def _store_seq(ref, g0, merge, chunk, x):
    ref[g0:g0 + merge] = x.reshape(merge, chunk, x.shape[1])


def _store_carry(ref, g0, merge, chunk, x):
    ref[g0:g0 + merge] = x.reshape(merge, chunk, x.shape[1])[:, chunk - CARRY_ROWS:chunk, :]


def _expand_states(compact_ref, full_ref):
    keep = _block_ones(GROUP_W, GROUP_W, HEAD_DIM, HEAD_DIM).astype(F32)
    for i in range(compact_ref.shape[0]):
        full_ref[i] = jnp.concatenate([compact_ref[i]] * N_HEADS, axis=1) * keep


def _compact_states(full_ref, compact_ref):
    for i in range(full_ref.shape[0]):
        s = full_ref[i]
        compact_ref[i] = jnp.concatenate(
            [s[h * HEAD_DIM:(h + 1) * HEAD_DIM, h * HEAD_DIM:(h + 1) * HEAD_DIM] for h in range(N_HEADS)], axis=0)


def _block_masks(n, chunk):
    row = _iota((n, n), 0)
    col = _iota((n, n), 1)
    same = (row >> _log2(chunk)) == (col >> _log2(chunk))
    return same & (col < row), same & (col <= row)


def _rwkv_body(p_ref, sh0_ref, s0_ref, mu_ref, w0_ref, w2_ref, a0_ref, a2_ref, g2_ref, kk_ref, ka_ref,
               rk_ref, lnw_ref, lnb_ref, o_ref, sh_ref, sc_ref, s_ref, *, chunk, merge, systems):
    @pl.when(pl.program_id(1) == 0)
    def _():
        _expand_states(s0_ref, s_ref)
        sh_ref[...] = sh0_ref[...]

    P = MIX_PASSES
    L = chunk
    R = merge * L
    n = N_HEADS * R
    bones = _block_ones(GROUP_W, GROUP_W, HEAD_DIM, HEAD_DIM)
    blockdiag = bones.astype(F32)
    masks = _head_masks(GROUP_W, HEAD_DIM)
    strict, incl = _block_masks(n, L)
    tri = _tri_incl(R, L)
    t_in = _iota((R, RWKV_COLS), 0) & (L - 1)

    def system(p, ext, states):
        prev = _shift_seq(p, ext, 1, t_in)
        xm = p + (prev - p) * mu_ref[...]
        r = xm[:, 0:GROUP_W]
        k = xm[:, GROUP_W:2 * GROUP_W]
        v = xm[:, 2 * GROUP_W:3 * GROUP_W]
        x_lr = xm[:, 3 * GROUP_W:RWKV_COLS]
        w_log = -_softplus(-(w0_ref[...] + _mm(jnp.tanh(x_lr), w2_ref[...]))) - 0.5
        lw = -jnp.exp(w_log)
        a_icl = _sigmoid(a0_ref[...] + _mm(x_lr, a2_ref[...]))
        gate = _mm(_sigmoid(x_lr), g2_ref[...])
        kk = k * kk_ref[...]
        kk = kk * lax.rsqrt(_mm(kk * kk, bones) + 1e-12)
        k = k * (1.0 + (a_icl - 1.0) * ka_ref[...])
        b_vec = kk * a_icl
        yield

        lg = _mm_cx(tri, lw)
        dec_out = jnp.exp(-lg)
        a_x = -kk * jnp.exp(lg - lw)
        r_x = r * jnp.exp(lg)
        a_s = _stack(a_x, masks)
        r_s = _stack(r_x, masks)
        v_s = _stack(v, masks)
        bk_s = jnp.concatenate([_stack(b_vec * dec_out, masks), _stack(k * dec_out, masks)], axis=0)
        yield
        ar_bk = _mm(jnp.concatenate([a_s, r_s], axis=0), bk_s, 1, 1, P)
        a_ab = jnp.where(strict, ar_bk[0:n, 0:n], 0.0)
        a_ak = jnp.where(strict, ar_bk[0:n, n:2 * n], 0.0)
        m_rb = jnp.where(incl, ar_bk[n:2 * n, 0:n], 0.0)
        m_rk = jnp.where(incl, ar_bk[n:2 * n, n:2 * n], 0.0)

        za, zr = [], []
        for g, s in enumerate(states):
            rows = slice(g * L, (g + 1) * L)
            za.append(_mm(a_x[rows], s, 1, 1, P))
            zr.append(_mm(r_x[rows], s, 1, 1, P))
        za = za[0] if merge == 1 else jnp.concatenate(za, axis=0)
        zr = zr[0] if merge == 1 else jnp.concatenate(zr, axis=0)
        rhs_u = _stack(za, masks) + _mm(a_ak, v_s, passes=P)
        yield
        t_inv = yield from _unit_lower_inverse(-a_ab, L, P)
        u = _mm(t_inv, rhs_u, passes=P)
        yield
        uv = jnp.concatenate([u, v_s], axis=0)
        y = zr + _unstack(_mm(jnp.concatenate([m_rb, m_rk], axis=1), uv, passes=P), masks, R)
        u_x = _unstack(u, masks, R)
        yield
        new_states = []
        for g, s in enumerate(states):
            rows = slice(g * L, (g + 1) * L)
            lg_g = lg[rows]
            lg_last = lg_g[L - 1:L]
            carry = jnp.exp(lg_last - lg_g)
            lhs = jnp.concatenate([u_x[rows], v[rows]], axis=0)
            rhs = jnp.concatenate([b_vec[rows] * carry, k[rows] * carry], axis=0)
            new_states.append(s * jnp.exp(lg_last) + blockdiag * _mm(lhs, rhs, 0, 0, P))

        inv_n = 1.0 / HEAD_DIM
        mean = _mm(y, bones) * inv_n
        yield
        yc = y - mean
        var = _mm(yc * yc, bones) * inv_n
        yield
        yn = yc * lax.rsqrt(var + RWKV_GN_EPS) * lnw_ref[...] + lnb_ref[...]
        bonus = _mm(r * k * rk_ref[...], bones) * v
        return (yn + bonus) * gate, new_states

    loaded = [(_seq_rows(p_ref, i * merge, merge), _carry_rows(sh_ref, i * merge, merge, L),
               [s_ref[i * merge + g] for g in range(merge)]) for i in range(systems)]
    results = _lockstep([system(*args) for args in loaded])
    for i, ((p, _, _), (out, new_states)) in enumerate(zip(loaded, results)):
        _store_carry(sh_ref, i * merge, merge, L, p)
        _store_seq(o_ref, i * merge, merge, L, out)
        for g, s in enumerate(new_states):
            s_ref[i * merge + g] = s

    @pl.when(pl.program_id(1) == pl.num_programs(1) - 1)
    def _():
        _compact_states(s_ref, sc_ref)


def _gdn_body(p_ref, cv0_ref, s0_ref, cw_ref, alog_ref, dtb_ref, nw_ref, o_ref, cv_ref, sc_ref, s_ref, *,
              chunk, merge, systems):
    @pl.when(pl.program_id(1) == 0)
    def _():
        _expand_states(s0_ref, s_ref)
        cv_ref[...] = cv0_ref[...]

    P = MIX_PASSES
    L = chunk
    R = merge * L
    n = N_HEADS * R
    bones = _block_ones(GROUP_W, GROUP_W, HEAD_DIM, HEAD_DIM)
    blockdiag = bones.astype(F32)
    masks = _head_masks(GROUP_W, HEAD_DIM)
    strict, incl = _block_masks(n, L)
    tri = _tri_incl(R, L)
    t_in = _iota((R, 3 * GROUP_W), 0) & (L - 1)
    src = _iota((LANE, GROUP_W), 0)
    dst = _iota((LANE, GROUP_W), 1) >> _log2(HEAD_DIM)
    spread_b = (src == dst).astype(F32).astype(BF16)
    spread_a = (src == dst + N_HEADS).astype(F32).astype(BF16)
    inv_w = 1.0 / HEAD_DIM

    def system(p, ext, states):
        qkv_in = p[:, 0:3 * GROUP_W]
        z = p[:, 3 * GROUP_W:4 * GROUP_W]
        ba = p[:, 4 * GROUP_W:GDN_P_W]
        qkv = _silu(_causal_conv(qkv_in, ext, cw_ref[...], t_in))
        q = qkv[:, 0:GROUP_W]
        k = qkv[:, GROUP_W:2 * GROUP_W]
        v = qkv[:, 2 * GROUP_W:3 * GROUP_W]
        q = q * lax.rsqrt(_mm(q * q, bones) + 1e-12) * (HEAD_DIM ** -0.5)
        k = k * lax.rsqrt(_mm(k * k, bones) + 1e-12)
        beta = _sigmoid(_mm_xc(ba, spread_b))
        g = -jnp.exp(alog_ref[...]) * _softplus(_mm_xc(ba, spread_a) + dtb_ref[...])
        yield

        gc = _mm_cx(tri, g)
        yield
        eg = jnp.exp(gc)
        k_s = _stack(k, masks)
        g_col = jnp.sum(_stack(gc, masks), axis=-1, keepdims=True) * inv_w
        beta_col = jnp.sum(_stack(beta, masks), axis=-1, keepdims=True) * inv_w
        g_row = jnp.transpose(jnp.broadcast_to(g_col, (n, LANE)))[0:1, :]
        decay = jnp.exp(jnp.minimum(g_col - g_row, 0.0))
        kq_k = _mm(jnp.concatenate([k_s, _stack(q, masks)], axis=0), k_s, 1, 1, P)
        yield
        a_mat = jnp.where(strict, beta_col * kq_k[0:n] * decay, 0.0)
        attn = jnp.where(incl, kq_k[n:2 * n] * decay, 0.0)
        rhs = jnp.concatenate([_stack(beta * eg * k, masks), _stack(beta * v, masks)], axis=1)
        t_inv = yield from _unit_lower_inverse(a_mat, L, P)
        wu = _mm(t_inv, rhs, passes=P)
        yield
        w_x = _unstack(wu[:, 0:GROUP_W], masks, R)
        qe = q * eg
        ws, qs = [], []
        for gi, s in enumerate(states):
            rows = slice(gi * L, (gi + 1) * L)
            ws.append(_mm(w_x[rows], s, passes=P))
            qs.append(_mm(qe[rows], s, passes=P))
        ws = ws[0] if merge == 1 else jnp.concatenate(ws, axis=0)
        qs = qs[0] if merge == 1 else jnp.concatenate(qs, axis=0)
        yield
        v_new = wu[:, GROUP_W:2 * GROUP_W] - _stack(ws, masks)
        o = qs + _unstack(_mm(attn, v_new, passes=P), masks, R)
        vn_x = _unstack(v_new, masks, R)
        yield
        new_states = []
        for gi, s in enumerate(states):
            rows = slice(gi * L, (gi + 1) * L)
            gc_g = gc[rows]
            g_last = gc_g[L - 1:L]
            k_hat = k[rows] * jnp.exp(g_last - gc_g)
            new_states.append(s * jnp.exp(g_last) + blockdiag * _mm(k_hat, vn_x[rows], 0, 0, P))

        o = o * lax.rsqrt(_mm(o * o, bones) * inv_w + NORM_EPS) * nw_ref[...] * _silu(z)
        return o, new_states

    loaded = [(_seq_rows(p_ref, i * merge, merge), _carry_rows(cv_ref, i * merge, merge, L),
               [s_ref[i * merge + g] for g in range(merge)]) for i in range(systems)]
    results = _lockstep([system(*args) for args in loaded])
    for i, ((p, _, _), (out, new_states)) in enumerate(zip(loaded, results)):
        _store_carry(cv_ref, i * merge, merge, L, p[:, 0:3 * GROUP_W])
        _store_seq(o_ref, i * merge, merge, L, out)
        for g, s in enumerate(new_states):
            s_ref[i * merge + g] = s

    @pl.when(pl.program_id(1) == pl.num_programs(1) - 1)
    def _():
        _compact_states(s_ref, sc_ref)


def _gla_body(p_ref, s0_ref, g2_ref, gb_ref, nw_ref, o_ref, s_ref, *, chunk, merge):
    @pl.when(pl.program_id(1) == 0)
    def _():
        s_ref[...] = s0_ref[...]

    P = MIX_PASSES
    L = chunk
    R = merge * L
    p = _seq_rows(p_ref, 0, merge)
    q = p[:, 0:GLA_QK_W] * (GLA_DK ** -0.5)
    k = p[:, GLA_QK_W:2 * GLA_QK_W]
    v = p[:, 2 * GLA_QK_W:2 * GLA_QK_W + GROUP_W]
    gt = p[:, 2 * GLA_QK_W + GROUP_W:2 * GLA_QK_W + 2 * GROUP_W]
    gd = p[:, 2 * GLA_QK_W + 2 * GROUP_W:GLA_P_W]
    gk = -_softplus(-(_mm(gd, g2_ref[...]) + gb_ref[...])) * (1.0 / GLA_TAU)
    bc = _mm_cx(_tri_incl(R, L), gk)
    spread = _block_ones(GLA_QK_W, GROUP_W, GLA_DK, HEAD_DIM)
    blockdiag = _block_ones(GROUP_W, GLA_QK_W, HEAD_DIM, GLA_DK).astype(F32)
    t_in = _iota((R, GROUP_W), 0) & (L - 1)

    def in_seq_row(x, j):
        x3 = x.reshape(merge, L, x.shape[1])
        return jnp.broadcast_to(x3[:, j:j + 1, :], x3.shape).reshape(x.shape)

    qe = q * jnp.exp(bc)
    o = []
    for g in range(merge):
        o.append(_mm(qe[g * L:(g + 1) * L], s_ref[g], 1, 1, P))
    o = o[0] if merge == 1 else jnp.concatenate(o, axis=0)
    for j in range(L):
        w = q * in_seq_row(k, j) * jnp.exp(jnp.minimum(bc - in_seq_row(bc, j), 0.0))
        o = o + jnp.where(t_in >= j, _mm(w, spread, passes=P), 0.0) * in_seq_row(v, j)
    for g in range(merge):
        rows = slice(g * L, (g + 1) * L)
        bc_g = bc[rows]
        b_last = bc_g[L - 1:L]
        k_hat = k[rows] * jnp.exp(b_last - bc_g)
        s_ref[g] = s_ref[g] * jnp.exp(b_last) + blockdiag * _mm(v[rows], k_hat, 0, 0, P)

    bones = _block_ones(GROUP_W, GROUP_W, HEAD_DIM, HEAD_DIM)
    o = o * lax.rsqrt(_mm(o * o, bones) * (1.0 / HEAD_DIM) + NORM_EPS) * nw_ref[...] * _silu(gt)
    _store_seq(o_ref, 0, merge, L, o)


def _rglru_body(p_ref, cv0_ref, h0_ref, cw_ref, cb_ref, wa_ref, ba_ref, wx_ref, bx_ref, lam_ref,
                o_ref, cv_ref, h_ref, *, chunk, merge):
    @pl.when(pl.program_id(1) == 0)
    def _():
        h_ref[...] = h0_ref[...]
        cv_ref[...] = cv0_ref[...]

    L = chunk
    R = merge * L
    p = _seq_rows(p_ref, 0, merge)
    xb = p[:, 0:GROUP_W]
    gate = p[:, GROUP_W:RG_P_W]
    t_in = _iota((R, GROUP_W), 0) & (L - 1)
    xc = _causal_conv(xb, _carry_rows(cv_ref, 0, merge, L), cw_ref[...], t_in) + cb_ref[...]
    _store_carry(cv_ref, 0, merge, L, xb)
    r = _sigmoid(_mm(xc, wa_ref[...]) + ba_ref[...])
    i = _sigmoid(_mm(xc, wx_ref[...]) + bx_ref[...])
    log_a = -RG_C * r * _softplus(-lam_ref[...])
    a = jnp.exp(log_a)
    b = jnp.sqrt(1.0 - jnp.exp(2.0 * log_a)) * (i * xc)
    d = 1
    while d < L:
        keep = t_in >= d
        a_sh = jnp.where(keep, pltpu.roll(a, d, 0), 1.0)
        b_sh = jnp.where(keep, pltpu.roll(b, d, 0), 0.0)
        b = a * b_sh + b
        a = a * a_sh
        d *= 2
    h3 = a.reshape(merge, L, GROUP_W) * h_ref[...] + b.reshape(merge, L, GROUP_W)
    h_ref[...] = h3[:, L - 1:L, :]
    h = h3.reshape(R, GROUP_W)
    gelu = 0.5 * gate * (1.0 + jnp.tanh(0.7978845608028654 * (gate + 0.044715 * gate * gate * gate)))
    _store_seq(o_ref, 0, merge, L, h * gelu)


def _mixer_params():
    return pltpu.CompilerParams(dimension_semantics=("parallel", "arbitrary"), vmem_limit_bytes=MIXER_VMEM_LIMIT)


def _seq_spec(seqs, rows, width):
    return pl.BlockSpec((seqs, rows, width), lambda b, c: (b, c, 0))


def _state_spec(seqs, rows, width):
    return pl.BlockSpec((seqs, rows, width), lambda b, c: (b, 0, 0))


def _param_spec(shape):
    return pl.BlockSpec(shape, lambda b, c: (0,) * len(shape))


def _delta_cfg(B, T):
    chunk = PROMPT_CHUNK if T % PROMPT_CHUNK == 0 else T
    merge = SYSTEM_ROWS // chunk
    systems = max(s for s in (8, 4, 2, 1) if B % (s * merge) == 0 and s * merge <= MAX_STATE_SEQS)
    return chunk, merge, systems


def _rwkv(p, shift0, s0, lp):
    B, T, _ = p.shape
    chunk, merge, systems = _delta_cfg(B, T)
    seqs = merge * systems
    row = lambda name: _param_spec((1, lp[name].shape[-1]))
    return pl.pallas_call(
        functools.partial(_rwkv_body, chunk=chunk, merge=merge, systems=systems),
        name="rwkv7",
        grid=(B // seqs, T // chunk),
        in_specs=[_seq_spec(seqs, chunk, RWKV_COLS), _state_spec(seqs, CARRY_ROWS, RWKV_COLS),
                  _state_spec(seqs, GROUP_W, HEAD_DIM),
                  row("rwkv_mu"), row("rwkv_w0"), _param_spec((LANE, GROUP_W)), row("rwkv_a0"),
                  _param_spec((LANE, GROUP_W)), _param_spec((LANE, GROUP_W)), row("rwkv_k_k"), row("rwkv_k_a"),
                  row("rwkv_r_k"), row("rwkv_ln_w"), row("rwkv_ln_b")],
        out_specs=[_seq_spec(seqs, chunk, GROUP_W), _state_spec(seqs, CARRY_ROWS, RWKV_COLS),
                   _state_spec(seqs, GROUP_W, HEAD_DIM)],
        out_shape=[jax.ShapeDtypeStruct((B, T, GROUP_W), F32), jax.ShapeDtypeStruct((B, CARRY_ROWS, RWKV_COLS), F32),
                   jax.ShapeDtypeStruct((B, GROUP_W, HEAD_DIM), F32)],
        scratch_shapes=[pltpu.VMEM((seqs, GROUP_W, GROUP_W), F32)],
        compiler_params=_mixer_params(),
    )(p, shift0, s0, lp["rwkv_mu"], lp["rwkv_w0"], lp["rwkv_w2"], lp["rwkv_a0"], lp["rwkv_a2"], lp["rwkv_g2"],
      lp["rwkv_k_k"], lp["rwkv_k_a"], lp["rwkv_r_k"], lp["rwkv_ln_w"], lp["rwkv_ln_b"])


def _gdn(p, conv0, s0, lp):
    B, T, _ = p.shape
    chunk, merge, systems = _delta_cfg(B, T)
    seqs = merge * systems
    return pl.pallas_call(
        functools.partial(_gdn_body, chunk=chunk, merge=merge, systems=systems),
        name="gdn",
        grid=(B // seqs, T // chunk),
        in_specs=[_seq_spec(seqs, chunk, GDN_P_W), _state_spec(seqs, CARRY_ROWS, 3 * GROUP_W),
                  _state_spec(seqs, GROUP_W, HEAD_DIM),
                  _param_spec((CONV_K, 3 * GROUP_W)), _param_spec((1, GROUP_W)), _param_spec((1, GROUP_W)),
                  _param_spec((1, GROUP_W))],
        out_specs=[_seq_spec(seqs, chunk, GROUP_W), _state_spec(seqs, CARRY_ROWS, 3 * GROUP_W),
                   _state_spec(seqs, GROUP_W, HEAD_DIM)],
        out_shape=[jax.ShapeDtypeStruct((B, T, GROUP_W), F32),
                   jax.ShapeDtypeStruct((B, CARRY_ROWS, 3 * GROUP_W), F32),
                   jax.ShapeDtypeStruct((B, GROUP_W, HEAD_DIM), F32)],
        scratch_shapes=[pltpu.VMEM((seqs, GROUP_W, GROUP_W), F32)],
        compiler_params=_mixer_params(),
    )(p, conv0, s0, lp["gdn_conv_w"], lp["gdn_A_log"], lp["gdn_dt_bias"], lp["gdn_norm_w"])


def _gla(p, s0, lp):
    B, T, _ = p.shape
    chunk = GLA_CHUNK if T % GLA_CHUNK == 0 else T
    merge = max(m for m in (16, 8, 4, 2, 1) if B % m == 0 and m * chunk <= GLA_ROWS)
    return pl.pallas_call(
        functools.partial(_gla_body, chunk=chunk, merge=merge),
        name="gla",
        grid=(B // merge, T // chunk),
        in_specs=[_seq_spec(merge, chunk, GLA_P_W), _state_spec(merge, GROUP_W, GLA_QK_W),
                  _param_spec((LANE, GLA_QK_W)), _param_spec((1, GLA_QK_W)), _param_spec((1, GROUP_W))],
        out_specs=[_seq_spec(merge, chunk, GROUP_W), _state_spec(merge, GROUP_W, GLA_QK_W)],
        out_shape=[jax.ShapeDtypeStruct((B, T, GROUP_W), F32), jax.ShapeDtypeStruct((B, GROUP_W, GLA_QK_W), F32)],
        compiler_params=_mixer_params(),
    )(p, s0, lp["gla_g2"], lp["gla_gb"], lp["gla_norm_w"])


def _rglru(p, conv0, h0, lp):
    B, T, _ = p.shape
    if T % RG_CHUNK == 0:
        chunk, merge = RG_CHUNK, 1
    else:
        assert T == CARRY_ROWS
        chunk, merge = T, B
    return pl.pallas_call(
        functools.partial(_rglru_body, chunk=chunk, merge=merge),
        name="rglru",
        grid=(B // merge, T // chunk),
        in_specs=[_seq_spec(merge, chunk, RG_P_W), _state_spec(merge, CARRY_ROWS, GROUP_W),
                  _state_spec(merge, 1, GROUP_W),
                  _param_spec((CONV_K, GROUP_W)), _param_spec((1, GROUP_W)),
                  _param_spec((GROUP_W, GROUP_W)), _param_spec((1, GROUP_W)),
                  _param_spec((GROUP_W, GROUP_W)), _param_spec((1, GROUP_W)), _param_spec((1, GROUP_W))],
        out_specs=[_seq_spec(merge, chunk, GROUP_W), _state_spec(merge, CARRY_ROWS, GROUP_W),
                   _state_spec(merge, 1, GROUP_W)],
        out_shape=[jax.ShapeDtypeStruct((B, T, GROUP_W), F32), jax.ShapeDtypeStruct((B, CARRY_ROWS, GROUP_W), F32),
                   jax.ShapeDtypeStruct((B, 1, GROUP_W), F32)],
        compiler_params=_mixer_params(),
    )(p, conv0, h0, lp["rg_conv_w"], lp["rg_conv_b"], lp["rg_wa"], lp["rg_ba"], lp["rg_wx"], lp["rg_bx"],
      lp["rg_lambda"])


def _block_diag(blocks):
    H, r, c = blocks.shape[-3:]
    lead = [(0, 0)] * (blocks.ndim - 3)
    rows = [jnp.pad(blocks[..., h, :, :], lead + [(0, 0), (h * c, (H - 1 - h) * c)]) for h in range(H)]
    return jnp.concatenate(rows, axis=-2)


def _diag_blocks(mat, H):
    r, c = mat.shape[-2] // H, mat.shape[-1] // H
    m = mat.reshape(*mat.shape[:-2], H, r, H, c)
    return jnp.stack([m[..., h, :, h, :] for h in range(H)], axis=-3)


def _pad_cols(w, offset, width):
    return jnp.pad(w, ((0, 0), (offset, width - offset - w.shape[1])))


def _pad_rows_to_lane(w, offset):
    return jnp.pad(w, ((offset, LANE - offset - w.shape[0]), (0, 0)))


def _dense_weights(raw):
    w_in = raw["w_in"]
    o_rg, o_gdn = RWKV_COLS + 2 * GROUP_W, RWKV_COLS + 2 * GROUP_W + 4 * GROUP_W + 2 * N_HEADS
    gdn = w_in[..., o_rg:o_gdn]
    gla = w_in[..., o_gdn:]
    pad = lambda w: jnp.pad(w, ((0, 0), (0, 0), (0, LANE - w.shape[-1])))
    w_in_p = jnp.concatenate([
        w_in[..., :o_rg],
        gdn[..., :4 * GROUP_W], pad(gdn[..., 4 * GROUP_W:]),
        gla[..., :2 * GLA_QK_W + 2 * GROUP_W], pad(gla[..., 2 * GLA_QK_W + 2 * GROUP_W:])], axis=-1)
    dense = {"w_in": w_in_p.astype(BF16), "w_out": raw["w_out"].astype(BF16)}
    for name in ("ffn1_gate", "ffn1_up", "ffn1_down", "ffn2_gate", "ffn2_up", "ffn2_down"):
        dense[name] = raw[name].astype(BF16)
    return dense


def _layer_params(raw, dense, i):
    g = lambda name: raw[name][i]
    row = lambda name: g(name).reshape(1, -1)
    lp = {
        "layer": i,
        "rwkv_w2": _pad_rows_to_lane(g("rwkv_w2"), 0),
        "rwkv_a2": _pad_rows_to_lane(g("rwkv_a2"), 32),
        "rwkv_g2": _pad_rows_to_lane(g("rwkv_g2"), 64),
        "rg_wa": _block_diag(g("rg_wa")), "rg_wx": _block_diag(g("rg_wx")),
        "gdn_conv_w": g("gdn_conv_w"), "rg_conv_w": g("rg_conv_w"),
        "gdn_A_log": jnp.repeat(g("gdn_A_log"), HEAD_DIM).reshape(1, -1),
        "gdn_dt_bias": jnp.repeat(g("gdn_dt_bias"), HEAD_DIM).reshape(1, -1),
        "gdn_norm_w": jnp.tile(g("gdn_norm_w"), N_HEADS).reshape(1, -1),
        "gla_norm_w": jnp.tile(g("gla_norm_w"), N_HEADS).reshape(1, -1),
        "gla_g2": _pad_rows_to_lane(g("gla_g2"), 0),
    }
    lp.update(dense)
    for name in ("norm_ffn1", "norm_mix", "norm_ffn2", "rwkv_mu", "rwkv_w0", "rwkv_a0", "rwkv_k_k", "rwkv_k_a",
                 "rwkv_r_k", "rwkv_ln_w", "rwkv_ln_b", "rg_conv_b", "rg_ba", "rg_bx", "rg_lambda", "gla_gb"):
        lp[name] = row(name)
    return lp


def _carry_tile(rows):
    return jnp.pad(rows, ((0, 0), (CARRY_ROWS - rows.shape[1], 0), (0, 0)))


def _run_trunk(x, states, lps, norm_final):
    B, T, _ = x.shape
    rows = B * T
    x2 = x.reshape(rows, D_MODEL)
    new_states = []
    for i, lp in enumerate(lps):
        if states is None:
            z = lambda *s: jnp.zeros((B,) + s, F32)
            sh0, rw0 = z(CARRY_ROWS, RWKV_COLS), z(GROUP_W, HEAD_DIM)
            rgc0, rgh0 = z(CARRY_ROWS, GROUP_W), z(1, GROUP_W)
            gdc0, gds0 = z(CARRY_ROWS, 3 * GROUP_W), z(GROUP_W, HEAD_DIM)
            gla0 = z(GROUP_W, GLA_QK_W)
        else:
            rw_shift, rw_S, rg_conv, rg_h, gdn_conv, gdn_S, gla_S = states[i]
            sh0 = _carry_tile(rw_shift[:, None, :])
            rw0 = rw_S.reshape(B, GROUP_W, HEAD_DIM)
            rgc0, rgh0 = _carry_tile(rg_conv), rg_h[:, None, :]
            gdc0, gds0 = _carry_tile(gdn_conv), gdn_S.reshape(B, GROUP_W, HEAD_DIM)
            gla0 = _block_diag(jnp.swapaxes(gla_S, -1, -2))
        x2, p_rw, p_rg, p_gdn, p_gla = _ffn_in(x2, lp)
        seq = lambda t: t.reshape(B, T, t.shape[-1])
        o_rw, sh1, rw1 = _rwkv(seq(p_rw), sh0, rw0, lp)
        o_rg, rgc1, rgh1 = _rglru(seq(p_rg), rgc0, rgh0, lp)
        o_gdn, gdc1, gds1 = _gdn(seq(p_gdn), gdc0, gds0, lp)
        o_gla, gla1 = _gla(seq(p_gla), gla0, lp)
        outs = [t.reshape(rows, GROUP_W) for t in (o_rw, o_rg, o_gdn, o_gla)]
        x2 = _out_ffn(x2, outs, lp, norm_final, final_norm=(i == len(lps) - 1))
        tail = CARRY_ROWS - (CONV_K - 1)
        heads = (B, N_HEADS, HEAD_DIM, HEAD_DIM)
        new_states.append((sh1[:, CARRY_ROWS - 1], rw1.reshape(heads), rgc1[:, tail:], rgh1[:, 0],
                           gdc1[:, tail:], gds1.reshape(heads),
                           jnp.swapaxes(_diag_blocks(gla1, N_HEADS), -1, -2)))
    stacked = tuple(jnp.stack([st[j] for st in new_states]) for j in range(len(new_states[0])))
    return x2.reshape(B, T, D_MODEL), stacked


_STACKED = ("norm_ffn1", "ffn1_gate", "ffn1_up", "ffn1_down", "norm_mix", "w_in", "w_out", "rwkv_mu", "rwkv_w0",
            "rwkv_w2", "rwkv_a0", "rwkv_a2", "rwkv_g2", "rwkv_k_k", "rwkv_k_a", "rwkv_r_k", "rwkv_ln_w",
            "rwkv_ln_b", "rg_conv_w", "rg_conv_b", "rg_wa", "rg_ba", "rg_wx", "rg_bx", "rg_lambda", "gdn_conv_w",
            "gdn_A_log", "gdn_dt_bias", "gdn_norm_w", "gla_g2", "gla_gb", "gla_norm_w", "norm_ffn2", "ffn2_gate",
            "ffn2_up", "ffn2_down")


def kernel(x_prompt, x_sample, state_rwkv_shift, state_rwkv_wkv, state_rglru_conv, state_rglru_h, state_gdn_conv, state_gdn_S, state_gla_S, norm_ffn1, ffn1_gate, ffn1_up, ffn1_down, norm_mix, w_in, w_out, rwkv_mu, rwkv_w0, rwkv_w2, rwkv_a0, rwkv_a2, rwkv_g2, rwkv_k_k, rwkv_k_a, rwkv_r_k, rwkv_ln_w, rwkv_ln_b, rg_conv_w, rg_conv_b, rg_wa, rg_ba, rg_wx, rg_bx, rg_lambda, gdn_conv_w, gdn_A_log, gdn_dt_bias, gdn_norm_w, gla_g2, gla_gb, gla_norm_w, norm_ffn2, ffn2_gate, ffn2_up, ffn2_down, norm_final):
    values = (norm_ffn1, ffn1_gate, ffn1_up, ffn1_down, norm_mix, w_in, w_out, rwkv_mu, rwkv_w0, rwkv_w2, rwkv_a0,
              rwkv_a2, rwkv_g2, rwkv_k_k, rwkv_k_a, rwkv_r_k, rwkv_ln_w, rwkv_ln_b, rg_conv_w, rg_conv_b, rg_wa,
              rg_ba, rg_wx, rg_bx, rg_lambda, gdn_conv_w, gdn_A_log, gdn_dt_bias, gdn_norm_w, gla_g2, gla_gb,
              gla_norm_w, norm_ffn2, ffn2_gate, ffn2_up, ffn2_down)
    raw = dict(zip(_STACKED, values))
    depth = w_in.shape[0]
    dense = _dense_weights(raw)
    lps = [_layer_params(raw, dense, i) for i in range(depth)]
    nf = norm_final.reshape(1, -1)

    y_prompt, new_p = _run_trunk(x_prompt, None, lps, nf)
    sample_in = (state_rwkv_shift, state_rwkv_wkv, state_rglru_conv, state_rglru_h,
                 state_gdn_conv, state_gdn_S, state_gla_S)
    y_sample, new_s = _run_trunk(x_sample, [tuple(s[i] for s in sample_in) for i in range(depth)], lps, nf)
    return (y_prompt, y_sample) + new_p + new_s
```
